```python
import jax, jax.numpy as jnp
from jax import lax
import numpy as np

D_MODEL = 1024
BATCH = 8
SEQ = 2048
DEPTH = 4

CHUNK = 64
D_MIX = D_MODEL
D_MLSTM = D_MIX // 2
D_POOL = D_MIX - D_MLSTM
N_MLSTM_HEADS = 4
HEAD_DIM = D_MLSTM // N_MLSTM_HEADS
POOL_WINDOWS = (2, 4, 8, 16)
N_POOL_GROUPS = len(POOL_WINDOWS)
POOL_GROUP_DIM = D_POOL // N_POOL_GROUPS
CONV_WIDTH = 4
D_FF = 4 * D_MODEL
OFF_Q = 0
OFF_K = OFF_Q + D_MLSTM
OFF_V = OFF_K + D_MLSTM
OFF_G = OFF_V + D_MLSTM
OFF_O = OFF_G + 2 * N_MLSTM_HEADS
OFF_P = OFF_O + D_MLSTM
D_IN = OFF_P + D_POOL
ALPHA = (2.0 * DEPTH) ** 0.25
BETA = (8.0 * DEPTH) ** -0.25
LN_EPS = 1e-5

kernel_name = "hymba_mlstm_multiscale_pool_deepnorm"


def layer_norm(x, g, b):
    xf = x.astype(jnp.float32)
    mu = xf.mean(-1, keepdims=True)
    var = jnp.square(xf - mu).mean(-1, keepdims=True)
    return ((xf - mu) * lax.rsqrt(var + LN_EPS) * g.astype(jnp.float32) + b.astype(jnp.float32)).astype(x.dtype)


def causal_conv(x, w):
    S = x.shape[1]
    xp = jnp.pad(x, ((0, 0), (CONV_WIDTH - 1, 0), (0, 0)))
    y = w[0] * xp[:, 0:S]
    for j in range(1, CONV_WIDTH):
        y = y + w[j] * xp[:, j:j + S]
    return y


def mlstm_chunkwise(q, k, v, i_pre, f_pre):
    B, S, H, Dh = q.shape
    NC = S // CHUNK

    def to_chunks(t):
        t = t.reshape((B, NC, CHUNK, H) + t.shape[3:])
        return jnp.moveaxis(t, 3, 1)

    q, k, v = to_chunks(q), to_chunks(k), to_chunks(v)
    i_pre = to_chunks(i_pre)
    logf = to_chunks(jax.nn.log_sigmoid(f_pre))
    a = jnp.cumsum(logf, axis=-1)
    g = a[..., -1]

    w = g[..., None] - a + i_pre
    m_loc = w.max(-1)
    e = jnp.exp(w - m_loc[..., None])
    C_loc = jnp.einsum('bhnl,bhnlv,bhnlk->bhnvk', e, v, k)
    n_loc = jnp.einsum('bhnl,bhnlk->bhnk', e, k)

    def step(carry, inp):
        C, n, m = carry
        g_c, m_c, C_c, n_c = inp
        m_new = jnp.maximum(g_c + m, m_c)
        s_old = jnp.exp(g_c + m - m_new)
        s_new = jnp.exp(m_c - m_new)
        C_new = s_old[..., None, None] * C + s_new[..., None, None] * C_c
        n_new = s_old[..., None] * n + s_new[..., None] * n_c
        return (C_new, n_new, m_new), (C, n, m)

    init = (jnp.zeros((B, H, Dh, Dh), jnp.float32),
            jnp.zeros((B, H, Dh), jnp.float32),
            jnp.zeros((B, H), jnp.float32))
    xs = (jnp.moveaxis(g, 2, 0), jnp.moveaxis(m_loc, 2, 0),
          jnp.moveaxis(C_loc, 2, 0), jnp.moveaxis(n_loc, 2, 0))
    _, (C_prev, n_prev, m_prev) = lax.scan(step, init, xs)
    C_prev = jnp.moveaxis(C_prev, 0, 2)
    n_prev = jnp.moveaxis(n_prev, 0, 2)
    m_prev = jnp.moveaxis(m_prev, 0, 2)

    causal = jnp.tril(jnp.ones((CHUNK, CHUNK), dtype=bool))
    Dlog = a[..., :, None] - a[..., None, :] + i_pre[..., None, :]
    Dlog = jnp.where(causal, Dlog, -jnp.inf)
    m_inter = a + m_prev[..., None]
    m_t = jnp.maximum(m_inter, Dlog.max(-1))
    P = jnp.exp(Dlog - m_t[..., None])
    s_inter = jnp.exp(m_inter - m_t)
    qk = jnp.einsum('bhnld,bhnsd->bhnls', q, k) * P
    num = (jnp.einsum('bhnls,bhnsv->bhnlv', qk, v)
           + s_inter[..., None] * jnp.einsum('bhnvk,bhnlk->bhnlv', C_prev, q))
    den = qk.sum(-1) + s_inter * jnp.einsum('bhnk,bhnlk->bhnl', n_prev, q)
    h = num / jnp.maximum(jnp.abs(den), jnp.exp(-m_t))[..., None]
    return jnp.moveaxis(h, 1, 3).reshape(B, S, H * Dh)


def multiscale_pool(p, w_pool, scale):
    B, S, _ = p.shape
    pf = p.astype(jnp.float32)
    cs = jnp.cumsum(pf, axis=1)
    count = jnp.arange(1, S + 1, dtype=jnp.float32)[:, None]
    outs = []
    for gi, win in enumerate(POOL_WINDOWS):
        lo, hi = gi * POOL_GROUP_DIM, (gi + 1) * POOL_GROUP_DIM
        c = cs[..., lo:hi]
        lagged = jnp.pad(c[:, :S - win], ((0, 0), (win, 0), (0, 0)))
        mean = (c - lagged) / jnp.minimum(count, float(win))
        outs.append(mean - pf[..., lo:hi])
    d = jnp.stack(outs, axis=2)
    y = jnp.einsum('bsgc,gcd->bsgd', d, w_pool.astype(jnp.float32)).reshape(B, S, D_POOL)
    return (y * scale.astype(jnp.float32)).astype(p.dtype)


def hybrid_mixer(x, w_in, b_gate, w_conv, hn_g, w_pool, pool_scale, w_out):
    B, S, _ = x.shape
    u = x @ w_in
    qk = jax.nn.silu(causal_conv(u[..., OFF_Q:OFF_V], w_conv)).astype(jnp.float32)
    q = qk[..., :D_MLSTM].reshape(B, S, N_MLSTM_HEADS, HEAD_DIM)
    k = qk[..., D_MLSTM:].reshape(B, S, N_MLSTM_HEADS, HEAD_DIM) * (HEAD_DIM ** -0.5)
    v = u[..., OFF_V:OFF_G].astype(jnp.float32).reshape(B, S, N_MLSTM_HEADS, HEAD_DIM)
    gates = (u[..., OFF_G:OFF_O] + b_gate).astype(jnp.float32)
    i_pre = gates[..., :N_MLSTM_HEADS]
    f_pre = gates[..., N_MLSTM_HEADS:]
    h = mlstm_chunkwise(q, k, v, i_pre, f_pre).reshape(B, S, N_MLSTM_HEADS, HEAD_DIM)
    mu = h.mean(-1, keepdims=True)
    var = jnp.square(h - mu).mean(-1, keepdims=True)
    h = ((h - mu) * lax.rsqrt(var + LN_EPS)).reshape(B, S, D_MLSTM) * hn_g.astype(jnp.float32)
    y_m = (jax.nn.sigmoid(u[..., OFF_O:OFF_P].astype(jnp.float32)) * h).astype(x.dtype)
    y_p = multiscale_pool(u[..., OFF_P:D_IN], w_pool, pool_scale)
    return jnp.concatenate([y_m, y_p], axis=-1) @ w_out


def squared_relu_mlp(x, w1, w2):
    return jnp.square(jax.nn.relu(x @ w1)) @ w2


def setup_inputs(seed: int = 0) -> dict:
    key = jax.random.key(seed)
    ks = jax.random.split(key, 16)
    L = DEPTH
    nrm = jax.random.normal
    x = nrm(ks[0], (BATCH, SEQ, D_MODEL), jnp.float32)
    w_in = nrm(ks[1], (L, D_MODEL, D_IN), jnp.float32) * D_MODEL ** -0.5
    b_i = 0.1 * nrm(ks[2], (L, N_MLSTM_HEADS), jnp.float32)
    b_f = jnp.linspace(3.0, 6.0, N_MLSTM_HEADS, dtype=jnp.float32)[None] + 0.1 * nrm(ks[3], (L, N_MLSTM_HEADS), jnp.float32)
    b_gate = jnp.concatenate([b_i, b_f], axis=-1)
    w_conv = nrm(ks[4], (L, CONV_WIDTH, 2 * D_MLSTM), jnp.float32) * CONV_WIDTH ** -0.5
    hn_g = 1.0 + 0.02 * nrm(ks[5], (L, D_MLSTM), jnp.float32)
    w_pool = nrm(ks[6], (L, N_POOL_GROUPS, POOL_GROUP_DIM, POOL_GROUP_DIM), jnp.float32) * POOL_GROUP_DIM ** -0.5
    pool_scale = 1.0 + 0.02 * nrm(ks[7], (L, D_POOL), jnp.float32)
    w_out = nrm(ks[8], (L, D_MIX, D_MODEL), jnp.float32) * (D_MIX ** -0.5) * BETA
    ln1_g = 1.0 + 0.02 * nrm(ks[9], (L, D_MODEL), jnp.float32)
    ln1_b = 0.02 * nrm(ks[10], (L, D_MODEL), jnp.float32)
    w_ff1 = nrm(ks[11], (L, D_MODEL, D_FF), jnp.float32) * D_MODEL ** -0.5
    w_ff2 = nrm(ks[12], (L, D_FF, D_MODEL), jnp.float32) * (D_FF ** -0.5) * BETA
    ln2_g = 1.0 + 0.02 * nrm(ks[13], (L, D_MODEL), jnp.float32)
    ln2_b = 0.02 * nrm(ks[14], (L, D_MODEL), jnp.float32)
    return {"x": x, "w_in": w_in, "b_gate": b_gate, "w_conv": w_conv, "hn_g": hn_g,
            "w_pool": w_pool, "pool_scale": pool_scale, "w_out": w_out,
            "ln1_g": ln1_g, "ln1_b": ln1_b, "w_ff1": w_ff1, "w_ff2": w_ff2,
            "ln2_g": ln2_g, "ln2_b": ln2_b}


def reference(x, w_in, b_gate, w_conv, hn_g, w_pool, pool_scale, w_out,
              ln1_g, ln1_b, w_ff1, w_ff2, ln2_g, ln2_b):
    for l in range(DEPTH):
        mix = hybrid_mixer(x, w_in[l], b_gate[l], w_conv[l], hn_g[l], w_pool[l], pool_scale[l], w_out[l])
        x = layer_norm(ALPHA * x + mix, ln1_g[l], ln1_b[l])
        x = layer_norm(ALPHA * x + squared_relu_mlp(x, w_ff1[l], w_ff2[l]), ln2_g[l], ln2_b[l])
    return x
```

```python
import functools

import jax
import jax.numpy as jnp
from jax import lax
from jax.experimental import pallas as pl
from jax.experimental.pallas import tpu as pltpu

D_MODEL = 1024
DEPTH = 4
N_HEADS = 4
HEAD_DIM = 128
D_MLSTM = N_HEADS * HEAD_DIM
POOL_WINDOWS = (2, 4, 8, 16)
POOL_GROUP_DIM = 128
D_POOL = len(POOL_WINDOWS) * POOL_GROUP_DIM
CONV_WIDTH = 4
D_FF = 4 * D_MODEL
ALPHA = (2.0 * DEPTH) ** 0.25
LN_EPS = 1e-5

OFF_Q = 0
OFF_K = OFF_Q + D_MLSTM
OFF_V = OFF_K + D_MLSTM
OFF_O = OFF_V + D_MLSTM
OFF_P = OFF_O + D_MLSTM
OFF_G = OFF_P + D_POOL
LANES = 128
D_IN_PAD = OFF_G + LANES

CONV_HALO = 8
POOL_HALO = 16

SEQ_TILE = 512
MLSTM_CHUNK = 256
MLP_TILE = 512
FF_CHUNK = 1024
VMEM_LIMIT_BYTES = 56 * 1024 * 1024

F32 = jnp.float32
BF16 = jnp.bfloat16


def _layer_norm_rows(z, g, b):
    mu = jnp.mean(z, axis=-1, keepdims=True)
    zc = z - mu
    var = jnp.mean(zc * zc, axis=-1, keepdims=True)
    return zc * lax.rsqrt(var + LN_EPS) * g + b


def _cumsum_rows(tri, x):
    p1 = x.astype(BF16)
    r1 = x - p1.astype(F32)
    p2 = r1.astype(BF16)
    p3 = (r1 - p2.astype(F32)).astype(BF16)
    out = jnp.dot(tri, p1, preferred_element_type=F32)
    out = out + jnp.dot(tri, p2, preferred_element_type=F32)
    return out + jnp.dot(tri, p3, preferred_element_type=F32)


def _mixer_kernel(x_ref, win_ref, bg_ref, wconv_ref, hng_ref, wpool_ref, pscale_ref,
                  wout_ref, g1_ref, b1_ref, o_ref,
                  qk_ext, p_ext, ycat, ct_ref, n_ref, m_ref, *, ts, chunk):
    t = pl.program_id(1)

    @pl.when(t == 0)
    def _():
        qk_ext[0:CONV_HALO, :] = jnp.zeros((CONV_HALO, 2 * D_MLSTM), F32)
        p_ext[0:POOL_HALO, :] = jnp.zeros((POOL_HALO, D_POOL), F32)
        ct_ref[...] = jnp.zeros_like(ct_ref)
        n_ref[...] = jnp.zeros_like(n_ref)
        m_ref[...] = jnp.zeros_like(m_ref)

    x = x_ref[...]
    u = jnp.dot(x.astype(BF16), win_ref[...], preferred_element_type=F32)

    qk_ext[CONV_HALO:CONV_HALO + ts, :] = u[:, OFF_Q:OFF_V]
    wc = wconv_ref[...]
    y = wc[0:1, :] * qk_ext[CONV_HALO - 3:CONV_HALO - 3 + ts, :]
    for j in range(1, CONV_WIDTH):
        off = CONV_HALO - (CONV_WIDTH - 1) + j
        y = y + wc[j:j + 1, :] * qk_ext[off:off + ts, :]
    qk_ext[0:CONV_HALO, :] = qk_ext[ts:ts + CONV_HALO, :]
    qk = jax.nn.silu(y)

    gates = u[:, OFF_G:OFF_G + LANES] + bg_ref[...]
    logf = jax.nn.log_sigmoid(gates)

    row = lax.broadcasted_iota(jnp.int32, (chunk, chunk), 0)
    col = lax.broadcasted_iota(jnp.int32, (chunk, chunk), 1)
    causal = row >= col
    tri = jnp.where(causal, 1.0, 0.0).astype(BF16)
    lane = lax.broadcasted_iota(jnp.int32, (chunk, LANES), 1)

    for c in range(ts // chunk):
        r0 = c * chunk
        g_c = gates[r0:r0 + chunk, :]
        a_c = _cumsum_rows(tri, logf[r0:r0 + chunk, :])
        zt = jnp.where(lane < N_HEADS, g_c, a_c).T
        for h in range(N_HEADS):
            hs = slice(h * HEAD_DIM, (h + 1) * HEAD_DIM)
            icol = g_c[:, h:h + 1]
            acol = a_c[:, N_HEADS + h:N_HEADS + h + 1]
            irow = zt[h:h + 1, :]
            arow = zt[N_HEADS + h:N_HEADS + h + 1, :]
            gtot = acol[chunk - 1:chunk, :]
            m_prev = m_ref[h:h + 1, 0:1]
            n_prev = n_ref[h:h + 1, :]
            ct_prev = ct_ref[h]

            q_h = qk[r0:r0 + chunk, hs]
            k_h = qk[r0:r0 + chunk, D_MLSTM + h * HEAD_DIM:D_MLSTM + (h + 1) * HEAD_DIM] * (HEAD_DIM ** -0.5)
            v_h = u[r0:r0 + chunk, OFF_V + h * HEAD_DIM:OFF_V + (h + 1) * HEAD_DIM]
            q_b = q_h.astype(BF16)
            k_b = k_h.astype(BF16)
            v_b = v_h.astype(BF16)

            w_row = gtot - arow + irow
            m_loc = jnp.max(w_row, axis=1, keepdims=True)
            e_col = jnp.exp(gtot - acol + icol - m_loc)

            dlog = jnp.where(causal, acol - arow + irow, -jnp.inf)
            m_inter = acol + m_prev
            m_t = jnp.maximum(m_inter, jnp.max(dlog, axis=1, keepdims=True))
            p = jnp.exp(dlog - m_t)
            s_inter = jnp.exp(m_inter - m_t)
            s = lax.dot_general(q_b, k_b, (((1,), (1,)), ((), ())), preferred_element_type=F32)
            sp = s * p
            num = (jnp.dot(sp.astype(BF16), v_b, preferred_element_type=F32)
                   + s_inter * jnp.dot(q_b, ct_prev.astype(BF16), preferred_element_type=F32))
            den = (jnp.sum(sp, axis=1, keepdims=True)
                   + s_inter * jnp.sum(q_h * n_prev, axis=1, keepdims=True))
            hh = num / jnp.maximum(jnp.abs(den), jnp.exp(-m_t))

            mu = jnp.mean(hh, axis=1, keepdims=True)
            hc = hh - mu
            var = jnp.mean(hc * hc, axis=1, keepdims=True)
            hn = hc * lax.rsqrt(var + LN_EPS) * hng_ref[:, hs]
            o_h = u[r0:r0 + chunk, OFF_O + h * HEAD_DIM:OFF_O + (h + 1) * HEAD_DIM]
            ycat[r0:r0 + chunk, hs] = (jax.nn.sigmoid(o_h) * hn).astype(BF16)

            m_new = jnp.maximum(gtot + m_prev, m_loc)
            s_old = jnp.exp(gtot + m_prev - m_new)
            s_new = jnp.exp(m_loc - m_new)
            ke = e_col * k_h
            c_loc = jnp.dot(ke.T.astype(BF16), v_b, preferred_element_type=F32)
            ct_ref[h] = s_old * ct_prev + s_new * c_loc
            n_ref[h:h + 1, :] = s_old * n_prev + s_new * jnp.sum(ke, axis=0, keepdims=True)
            m_ref[h:h + 1, :] = jnp.broadcast_to(m_new, (1, LANES))

    p_ext[POOL_HALO:POOL_HALO + ts, :] = u[:, OFF_P:OFF_P + D_POOL]
    count = (t * ts + 1 + lax.broadcasted_iota(jnp.int32, (ts, 1), 0)).astype(F32)
    for gi, win in enumerate(POOL_WINDOWS):
        ls = slice(gi * POOL_GROUP_DIM, (gi + 1) * POOL_GROUP_DIM)
        cur = p_ext[POOL_HALO:POOL_HALO + ts, ls]
        wsum = cur
        for j in range(1, win):
            wsum = wsum + p_ext[POOL_HALO - j:POOL_HALO - j + ts, ls]
        d = wsum / jnp.minimum(count, float(win)) - cur
        yp = jnp.dot(d.astype(BF16), wpool_ref[gi], preferred_element_type=F32) * pscale_ref[:, ls]
        ycat[:, D_MLSTM + gi * POOL_GROUP_DIM:D_MLSTM + (gi + 1) * POOL_GROUP_DIM] = yp.astype(BF16)
    p_ext[0:POOL_HALO, :] = p_ext[ts:ts + POOL_HALO, :]

    mix = jnp.dot(ycat[...], wout_ref[...], preferred_element_type=F32)
    o_ref[...] = _layer_norm_rows(ALPHA * x + mix, g1_ref[...], b1_ref[...])


def _mlp_kernel(x_ref, w1_ref, w2_ref, g_ref, b_ref, o_ref):
    x = x_ref[...]
    xb = x.astype(BF16)
    acc = None
    for c in range(D_FF // FF_CHUNK):
        cs = slice(c * FF_CHUNK, (c + 1) * FF_CHUNK)
        hid = jnp.dot(xb, w1_ref[:, cs], preferred_element_type=F32)
        act = jnp.square(jnp.maximum(hid, 0.0)).astype(BF16)
        part = jnp.dot(act, w2_ref[cs, :], preferred_element_type=F32)
        acc = part if acc is None else acc + part
    o_ref[...] = _layer_norm_rows(ALPHA * x + acc, g_ref[...], b_ref[...])


def _const_spec(shape):
    return pl.BlockSpec(shape, lambda *_: (0,) * len(shape), pipeline_mode=pl.Buffered(1))


def _mixer_call(x2d, win, bg, wconv, hng, wpool, pscale, wout, g1, b1, *, batch, seq):
    ts = SEQ_TILE
    nt = seq // ts
    row_spec = pl.BlockSpec((ts, D_MODEL), lambda b, t: (b * nt + t, 0))
    return pl.pallas_call(
        functools.partial(_mixer_kernel, ts=ts, chunk=MLSTM_CHUNK),
        grid=(batch, nt),
        in_specs=[
            row_spec,
            _const_spec((D_MODEL, D_IN_PAD)),
            _const_spec((1, LANES)),
            _const_spec((CONV_WIDTH, 2 * D_MLSTM)),
            _const_spec((1, D_MLSTM)),
            _const_spec((len(POOL_WINDOWS), POOL_GROUP_DIM, POOL_GROUP_DIM)),
            _const_spec((1, D_POOL)),
            _const_spec((D_MODEL, D_MODEL)),
            _const_spec((1, D_MODEL)),
            _const_spec((1, D_MODEL)),
        ],
        out_specs=row_spec,
        out_shape=jax.ShapeDtypeStruct(x2d.shape, F32),
        scratch_shapes=[
            pltpu.VMEM((ts + CONV_HALO, 2 * D_MLSTM), F32),
            pltpu.VMEM((ts + POOL_HALO, D_POOL), F32),
            pltpu.VMEM((ts, D_MODEL), BF16),
            pltpu.VMEM((N_HEADS, HEAD_DIM, HEAD_DIM), F32),
            pltpu.VMEM((8, HEAD_DIM), F32),
            pltpu.VMEM((8, LANES), F32),
        ],
        compiler_params=pltpu.CompilerParams(
            dimension_semantics=("arbitrary", "arbitrary"),
            vmem_limit_bytes=VMEM_LIMIT_BYTES),
        name="mixer",
    )(x2d, win, bg, wconv, hng, wpool, pscale, wout, g1, b1)


def _mlp_call(x2d, w1, w2, g, b):
    tm = MLP_TILE
    row_spec = pl.BlockSpec((tm, D_MODEL), lambda i: (i, 0))
    return pl.pallas_call(
        _mlp_kernel,
        grid=(x2d.shape[0] // tm,),
        in_specs=[
            row_spec,
            _const_spec((D_MODEL, D_FF)),
            _const_spec((D_FF, D_MODEL)),
            _const_spec((1, D_MODEL)),
            _const_spec((1, D_MODEL)),
        ],
        out_specs=row_spec,
        out_shape=jax.ShapeDtypeStruct(x2d.shape, F32),
        compiler_params=pltpu.CompilerParams(
            dimension_semantics=("arbitrary",),
            vmem_limit_bytes=VMEM_LIMIT_BYTES),
        name="mlp",
    )(x2d, w1, w2, g, b)


def kernel(x, w_in, b_gate, w_conv, hn_g, w_pool, pool_scale, w_out,
           ln1_g, ln1_b, w_ff1, w_ff2, ln2_g, ln2_b):
    batch, seq, d_model = x.shape
    assert d_model == D_MODEL and seq % SEQ_TILE == 0 and SEQ_TILE % MLSTM_CHUNK == 0
    assert (batch * seq) % MLP_TILE == 0
    n_gate = 2 * N_HEADS
    off_gate_src = 3 * D_MLSTM
    win = jnp.concatenate([
        w_in[:, :, :off_gate_src],
        w_in[:, :, off_gate_src + n_gate:],
        w_in[:, :, off_gate_src:off_gate_src + n_gate],
        jnp.zeros((DEPTH, D_MODEL, LANES - n_gate), w_in.dtype)], axis=-1).astype(BF16)
    bg = jnp.pad(b_gate, ((0, 0), (0, LANES - n_gate)))[:, None, :]
    wpool = w_pool.astype(BF16)
    wout = w_out.astype(BF16)
    w1 = w_ff1.astype(BF16)
    w2 = w_ff2.astype(BF16)

    h = x.reshape(batch * seq, d_model)
    for l in range(DEPTH):
        h = _mixer_call(h, win[l], bg[l], w_conv[l], hn_g[l][None], wpool[l],
                        pool_scale[l][None], wout[l], ln1_g[l][None], ln1_b[l][None],
                        batch=batch, seq=seq)
        h = _mlp_call(h, w1[l], w2[l], ln2_g[l][None], ln2_b[l][None])
    return h.reshape(batch, seq, d_model)
```

```python
import functools

import jax
import jax.numpy as jnp
from jax import lax
from jax.experimental import pallas as pl
from jax.experimental.pallas import tpu as pltpu

D_MODEL = 1024
DEPTH = 4
N_HEADS = 4
HEAD_DIM = 128
D_MLSTM = N_HEADS * HEAD_DIM
POOL_WINDOWS = (2, 4, 8, 16)
POOL_GROUP_DIM = 128
D_POOL = len(POOL_WINDOWS) * POOL_GROUP_DIM
CONV_WIDTH = 4
D_FF = 4 * D_MODEL
ALPHA = (2.0 * DEPTH) ** 0.25
LN_EPS = 1e-5

OFF_Q = 0
OFF_K = OFF_Q + D_MLSTM
OFF_V = OFF_K + D_MLSTM
OFF_O = OFF_V + D_MLSTM
OFF_P = OFF_O + D_MLSTM
OFF_G = OFF_P + D_POOL
LANES = 128
D_IN_PAD = OFF_G + LANES

CONV_HALO = 8
POOL_HALO = 16

SEQ_TILE = 512
MLSTM_CHUNK = 256
FF_CHUNK = 512
VMEM_LIMIT_BYTES = 60000 * 1024

F32 = jnp.float32
BF16 = jnp.bfloat16


def _layer_norm_rows(z, g, b):
    mu = jnp.mean(z, axis=-1, keepdims=True)
    zc = z - mu
    var = jnp.mean(zc * zc, axis=-1, keepdims=True)
    return zc * lax.rsqrt(var + LN_EPS) * g + b


def _cumsum_rows(tri, x):
    p1 = x.astype(BF16)
    r1 = x - p1.astype(F32)
    p2 = r1.astype(BF16)
    p3 = (r1 - p2.astype(F32)).astype(BF16)
    out = jnp.dot(tri, p1, preferred_element_type=F32)
    out = out + jnp.dot(tri, p2, preferred_element_type=F32)
    return out + jnp.dot(tri, p3, preferred_element_type=F32)


def _mlp_phases(x1_buf, w1_ref, w2_ref, g2_ref, b2_ref, o_ref):
    x1 = x1_buf[...]
    xb = x1.astype(BF16)
    acc = None
    for c in range(D_FF // FF_CHUNK):
        cs = slice(c * FF_CHUNK, (c + 1) * FF_CHUNK)
        hid = jnp.dot(xb, w1_ref[:, cs], preferred_element_type=F32)
        yield
        act = jnp.square(jnp.maximum(hid, 0.0)).astype(BF16)
        part = jnp.dot(act, w2_ref[cs, :], preferred_element_type=F32)
        acc = part if acc is None else acc + part
        yield
    o_ref[...] = _layer_norm_rows(ALPHA * x1 + acc, g2_ref[...], b2_ref[...])
    yield


def _mixer_phases(x_ref, t_glob, win_ref, bg_ref, wconv_ref, hng_ref, wpool_ref, pscale_ref,
                  wout_ref, g1_ref, b1_ref, tri_ref, bias_ref, x1_buf,
                  qk_ext, p_ext, vb_ref, o_buf, z_ref, zt_ref, ycat, ct_ref, m_ref,
                  *, ts, chunk):
    n_chunks = ts // chunk
    xb = x_ref[...].astype(BF16)

    qk_ext[CONV_HALO:CONV_HALO + ts, :] = jnp.dot(
        xb, win_ref[:, OFF_Q:OFF_V], preferred_element_type=F32)
    yield
    u_vo = jnp.dot(xb, win_ref[:, OFF_V:OFF_P], preferred_element_type=F32)
    vb_ref[...] = u_vo[:, :D_MLSTM].astype(BF16)
    o_buf[...] = u_vo[:, D_MLSTM:]
    yield
    u_pg = jnp.dot(xb, win_ref[:, OFF_P:D_IN_PAD], preferred_element_type=F32)
    p_ext[POOL_HALO:POOL_HALO + ts, :] = u_pg[:, :D_POOL]
    gates = u_pg[:, D_POOL:] + bg_ref[...]
    logf = jax.nn.log_sigmoid(gates)
    lane = lax.broadcasted_iota(jnp.int32, (chunk, LANES), 1)
    for c in range(n_chunks):
        rs = slice(c * chunk, (c + 1) * chunk)
        a_c = _cumsum_rows(tri_ref[...], logf[rs, :])
        z_c = jnp.where(lane < N_HEADS, gates[rs, :], a_c)
        z_ref[rs, :] = z_c
        zt_ref[:, rs] = z_c.T
    yield

    wc = wconv_ref[...]
    conv_tail = qk_ext[ts:ts + CONV_HALO, :]
    for c in reversed(range(n_chunks)):
        blk = qk_ext[c * chunk:c * chunk + chunk + CONV_HALO, :]
        y = wc[0:1, :] * pltpu.roll(blk, CONV_WIDTH - 1, 0)[CONV_HALO:, :]
        for j in range(1, CONV_WIDTH - 1):
            y = y + wc[j:j + 1, :] * pltpu.roll(blk, CONV_WIDTH - 1 - j, 0)[CONV_HALO:, :]
        y = y + wc[CONV_WIDTH - 1:CONV_WIDTH, :] * blk[CONV_HALO:, :]
        act = jax.nn.silu(y)
        rows = slice(CONV_HALO + c * chunk, CONV_HALO + (c + 1) * chunk)
        qk_ext[rows, OFF_Q:OFF_K] = act[:, OFF_Q:OFF_K]
        qk_ext[rows, OFF_K:OFF_V] = act[:, OFF_K:OFF_V] * (HEAD_DIM ** -0.5)
        yield
    qk_ext[0:CONV_HALO, :] = conv_tail

    ones_col = jnp.where(lane == 0, 1.0, 0.0).astype(BF16)
    for c in range(n_chunks):
        rs = slice(c * chunk, (c + 1) * chunk)
        rows = slice(CONV_HALO + c * chunk, CONV_HALO + (c + 1) * chunk)
        for h in range(N_HEADS):
            hs = slice(h * HEAD_DIM, (h + 1) * HEAD_DIM)
            icol = z_ref[rs, h:h + 1]
            acol = z_ref[rs, N_HEADS + h:N_HEADS + h + 1]
            irow = zt_ref[h:h + 1, rs]
            arow = zt_ref[N_HEADS + h:N_HEADS + h + 1, rs]
            gtot = acol[chunk - 1:chunk, :]
            m_prev = m_ref[h:h + 1, 0:1]
            ct_prev = ct_ref[h]

            q_b = qk_ext[rows, OFF_Q + h * HEAD_DIM:OFF_Q + (h + 1) * HEAD_DIM].astype(BF16)
            k_h = qk_ext[rows, OFF_K + h * HEAD_DIM:OFF_K + (h + 1) * HEAD_DIM]
            k_b = k_h.astype(BF16)
            v_aug = jnp.concatenate([vb_ref[rs, hs], ones_col], axis=1)

            w_row = gtot - arow + irow
            m_loc = jnp.max(w_row, axis=1, keepdims=True)
            e_col = jnp.exp(gtot - acol + icol - m_loc)

            dlog = acol - arow + irow + bias_ref[...]
            m_inter = acol + m_prev
            m_t = jnp.maximum(m_inter, jnp.max(dlog, axis=1, keepdims=True))
            p = jnp.exp(dlog - m_t)
            s_inter = jnp.exp(m_inter - m_t)
            s = lax.dot_general(q_b, k_b, (((1,), (1,)), ((), ())), preferred_element_type=F32)
            sp_b = (s * p).astype(BF16)
            tot = (jnp.dot(sp_b, v_aug, preferred_element_type=F32)
                   + s_inter * jnp.dot(q_b, ct_prev.astype(BF16), preferred_element_type=F32))
            num = tot[:, :HEAD_DIM]
            den = tot[:, HEAD_DIM:HEAD_DIM + 1]
            hh = num / jnp.maximum(jnp.abs(den), jnp.exp(-m_t))

            mu = jnp.mean(hh, axis=1, keepdims=True)
            hc = hh - mu
            var = jnp.mean(hc * hc, axis=1, keepdims=True)
            hn = hc * lax.rsqrt(var + LN_EPS) * hng_ref[:, hs]
            ycat[rs, hs] = (jax.nn.sigmoid(o_buf[rs, hs]) * hn).astype(BF16)

            m_new = jnp.maximum(gtot + m_prev, m_loc)
            s_old = jnp.exp(gtot + m_prev - m_new)
            s_new = jnp.exp(m_loc - m_new)
            ke_t = (e_col * k_h).T.astype(BF16)
            ct_ref[h] = s_old * ct_prev + s_new * jnp.dot(ke_t, v_aug, preferred_element_type=F32)
            m_ref[h:h + 1, :] = jnp.broadcast_to(m_new, (1, LANES))
            yield

    ext = p_ext[...]
    p_ext[0:POOL_HALO, :] = ext[ts:ts + POOL_HALO, :]
    count = (t_glob * ts + 1 + lax.broadcasted_iota(jnp.int32, (ts, 1), 0)).astype(F32)
    wsum = ext
    shift = 1
    for gi, win in enumerate(POOL_WINDOWS):
        while shift < win:
            wsum = wsum + pltpu.roll(wsum, shift, 0)
            shift *= 2
        ls = slice(gi * POOL_GROUP_DIM, (gi + 1) * POOL_GROUP_DIM)
        cur = ext[POOL_HALO:, ls]
        d = wsum[POOL_HALO:, :POOL_GROUP_DIM] / jnp.minimum(count, float(win)) - cur
        yp = jnp.dot(d.astype(BF16), wpool_ref[gi], preferred_element_type=F32) * pscale_ref[:, ls]
        ycat[:, D_MLSTM + gi * POOL_GROUP_DIM:D_MLSTM + (gi + 1) * POOL_GROUP_DIM] = yp.astype(BF16)
        wsum = wsum[:, POOL_GROUP_DIM:]
        yield

    mix = jnp.dot(ycat[...], wout_ref[...], preferred_element_type=F32)
    x1_buf[...] = _layer_norm_rows(ALPHA * x_ref[...] + mix, g1_ref[...], b1_ref[...])
    yield


def _layer_kernel(x_ref, win_ref, bg_ref, wconv_ref, hng_ref, wpool_ref, pscale_ref,
                  wout_ref, g1_ref, b1_ref, w1_ref, w2_ref, g2_ref, b2_ref, tri_ref, bias_ref,
                  o_ref, x1_buf, qk_ext, p_ext, vb_ref, o_buf, z_ref, zt_ref, ycat, ct_ref, m_ref,
                  *, ts, chunk, nt):
    i = pl.program_id(0)
    t_glob = lax.rem(i, nt)

    @pl.when(i == 0)
    def _():
        x1_buf[...] = jnp.zeros_like(x1_buf)

    @pl.when(t_glob == 0)
    def _():
        qk_ext[0:CONV_HALO, :] = jnp.zeros((CONV_HALO, 2 * D_MLSTM), F32)
        p_ext[0:POOL_HALO, :] = jnp.zeros((POOL_HALO, D_POOL), F32)
        ct_ref[...] = jnp.zeros_like(ct_ref)
        m_ref[...] = jnp.zeros_like(m_ref)

    mixer = _mixer_phases(
        x_ref, t_glob, win_ref, bg_ref, wconv_ref, hng_ref, wpool_ref, pscale_ref,
        wout_ref, g1_ref, b1_ref, tri_ref, bias_ref, x1_buf,
        qk_ext, p_ext, vb_ref, o_buf, z_ref, zt_ref, ycat, ct_ref, m_ref, ts=ts, chunk=chunk)
    mlp = _mlp_phases(x1_buf, w1_ref, w2_ref, g2_ref, b2_ref, o_ref)
    live = [mlp, mixer]
    while live:
        for gen in list(live):
            try:
                next(gen)
            except StopIteration:
                live.remove(gen)


def _layer_spec(shape, layer):
    return pl.BlockSpec((None,) + shape, lambda i: (layer,) + (0,) * len(shape),
                        pipeline_mode=pl.Buffered(1))


def _const_spec(shape):
    return pl.BlockSpec(shape, lambda i: (0,) * len(shape), pipeline_mode=pl.Buffered(1))


def _layer_call(x2d, params, consts, layer, *, seq):
    ts = SEQ_TILE
    chunk = MLSTM_CHUNK
    n_tiles = x2d.shape[0] // ts
    shapes = [
        (D_MODEL, D_IN_PAD), (1, LANES), (CONV_WIDTH, 2 * D_MLSTM), (1, D_MLSTM),
        (len(POOL_WINDOWS), POOL_GROUP_DIM, POOL_GROUP_DIM), (1, D_POOL),
        (D_MODEL, D_MODEL), (1, D_MODEL), (1, D_MODEL),
        (D_MODEL, D_FF), (D_FF, D_MODEL), (1, D_MODEL), (1, D_MODEL),
    ]
    x_spec = pl.BlockSpec((ts, D_MODEL), lambda i: (jnp.minimum(i, n_tiles - 1), 0))
    o_spec = pl.BlockSpec((ts, D_MODEL), lambda i: (jnp.maximum(i - 1, 0), 0))
    return pl.pallas_call(
        functools.partial(_layer_kernel, ts=ts, chunk=chunk, nt=seq // ts),
        grid=(n_tiles + 1,),
        in_specs=([x_spec] + [_layer_spec(s, layer) for s in shapes]
                  + [_const_spec((chunk, chunk)), _const_spec((chunk, chunk))]),
        out_specs=o_spec,
        out_shape=jax.ShapeDtypeStruct(x2d.shape, F32),
        scratch_shapes=[
            pltpu.VMEM((ts, D_MODEL), F32),
            pltpu.VMEM((ts + CONV_HALO, 2 * D_MLSTM), F32),
            pltpu.VMEM((ts + POOL_HALO, D_POOL), F32),
            pltpu.VMEM((ts, D_MLSTM), BF16),
            pltpu.VMEM((ts, D_MLSTM), F32),
            pltpu.VMEM((ts, LANES), F32),
            pltpu.VMEM((LANES, ts), F32),
            pltpu.VMEM((ts, D_MODEL), BF16),
            pltpu.VMEM((N_HEADS, HEAD_DIM, 2 * HEAD_DIM), F32),
            pltpu.VMEM((8, LANES), F32),
        ],
        compiler_params=pltpu.CompilerParams(
            dimension_semantics=("arbitrary",),
            vmem_limit_bytes=VMEM_LIMIT_BYTES),
        name=f"layer{layer}",
    )(x2d, *params, *consts)


def kernel(x, w_in, b_gate, w_conv, hn_g, w_pool, pool_scale, w_out,
           ln1_g, ln1_b, w_ff1, w_ff2, ln2_g, ln2_b):
    batch, seq, d_model = x.shape
    assert d_model == D_MODEL and seq % SEQ_TILE == 0 and SEQ_TILE % MLSTM_CHUNK == 0
    n_gate = 2 * N_HEADS
    off_gate_src = 3 * D_MLSTM
    win = jnp.concatenate([
        w_in[:, :, :off_gate_src].astype(BF16),
        w_in[:, :, off_gate_src + n_gate:].astype(BF16),
        w_in[:, :, off_gate_src:off_gate_src + n_gate].astype(BF16),
        jnp.zeros((DEPTH, D_MODEL, LANES - n_gate), BF16)], axis=-1)
    params = (
        win,
        jnp.pad(b_gate, ((0, 0), (0, LANES - n_gate)))[:, None, :],
        w_conv, hn_g[:, None, :], w_pool.astype(BF16), pool_scale[:, None, :],
        w_out.astype(BF16), ln1_g[:, None, :], ln1_b[:, None, :],
        w_ff1.astype(BF16), w_ff2.astype(BF16), ln2_g[:, None, :], ln2_b[:, None, :],
    )
    causal = jnp.tril(jnp.ones((MLSTM_CHUNK, MLSTM_CHUNK), dtype=bool))
    consts = (causal.astype(BF16), jnp.where(causal, 0.0, -jnp.inf).astype(F32))
    h = x.reshape(batch * seq, d_model)
    for layer in range(DEPTH):
        h = _layer_call(h, params, consts, layer, seq=seq)
    return h.reshape(batch, seq, d_model)
```

```python
import functools

import jax
import jax.numpy as jnp
from jax import lax
from jax.experimental import pallas as pl
from jax.experimental.pallas import tpu as pltpu

D_MODEL = 1024
DEPTH = 4
N_HEADS = 4
HEAD_DIM = 128
D_MLSTM = N_HEADS * HEAD_DIM
POOL_WINDOWS = (2, 4, 8, 16)
POOL_GROUP_DIM = 128
D_POOL = len(POOL_WINDOWS) * POOL_GROUP_DIM
CONV_WIDTH = 4
D_FF = 4 * D_MODEL
ALPHA = (2.0 * DEPTH) ** 0.25
LN_EPS = 1e-5

OFF_Q = 0
OFF_K = OFF_Q + D_MLSTM
OFF_V = OFF_K + D_MLSTM
OFF_O = OFF_V + D_MLSTM
OFF_P = OFF_O + D_MLSTM
OFF_G = OFF_P + D_POOL
LANES = 128
MXU_DIM = 256
UP_PIECE = 512
D_IN_PAD = OFF_G + LANES

CONV_HALO = 8
POOL_HALO = 16

SEQ_TILE = 512
MLSTM_CHUNK = 256
VMEM_LIMIT_BYTES = 60000 * 1024

F32 = jnp.float32
BF16 = jnp.bfloat16


def _layer_norm_rows(z, g, b):
    mu = jnp.mean(z, axis=-1, keepdims=True)
    zc = z - mu
    var = jnp.mean(zc * zc, axis=-1, keepdims=True)
    return zc * lax.rsqrt(var + LN_EPS) * g + b


def _split3_bf16(x):
    p1 = x.astype(BF16)
    r1 = x - p1.astype(F32)
    p2 = r1.astype(BF16)
    p3 = (r1 - p2.astype(F32)).astype(BF16)
    return p1, p2, p3


def _mlp_down_phases(act_buf, x1_buf, w2_ref, g2_ref, b2_ref, o_ref):
    x1 = x1_buf[...]
    yield
    outs = []
    for j in range(D_MODEL // MXU_DIM):
        js = slice(j * MXU_DIM, (j + 1) * MXU_DIM)
        outs.append(jnp.dot(act_buf[...], w2_ref[:, js], preferred_element_type=F32))
        yield
    y = jnp.concatenate(outs, axis=1)
    o_ref[...] = _layer_norm_rows(ALPHA * x1 + y, g2_ref[...], b2_ref[...])
    yield


def _mlp_up_phases(z1_buf, g1_ref, b1_ref, w1_ref, x1_buf, x1b_buf, act_buf):
    x1 = _layer_norm_rows(z1_buf[...], g1_ref[...], b1_ref[...])
    x1_buf[...] = x1
    x1b_buf[...] = x1.astype(BF16)
    yield

    def act_of(hid):
        return jnp.square(jnp.maximum(hid, 0.0)).astype(BF16)

    n_up = D_FF // UP_PIECE
    hid_prev = None
    for c in range(n_up):
        cs = slice(c * UP_PIECE, (c + 1) * UP_PIECE)
        hid = jnp.dot(x1b_buf[...], w1_ref[:, cs], preferred_element_type=F32)
        if hid_prev is not None:
            act_buf[:, (c - 1) * UP_PIECE:c * UP_PIECE] = act_of(hid_prev)
        hid_prev = hid
        yield
    act_buf[:, (n_up - 1) * UP_PIECE:] = act_of(hid_prev)
    yield


def _head_phases(c, h, hng_ref, bias_ref, qk_ext, vb_ref, o_buf, z_ref, zt_ref,
                 ycat, ct_ref, m_ref, *, chunk):
    rs = slice(c * chunk, (c + 1) * chunk)
    rows = slice(CONV_HALO + c * chunk, CONV_HALO + (c + 1) * chunk)
    hs = slice(h * HEAD_DIM, (h + 1) * HEAD_DIM)
    lane = lax.broadcasted_iota(jnp.int32, (chunk, LANES), 1)
    ones_col = jnp.where(lane == 0, 1.0, 0.0).astype(BF16)

    icol = z_ref[rs, h:h + 1]
    acol = z_ref[rs, N_HEADS + h:N_HEADS + h + 1]
    irow = zt_ref[h:h + 1, rs]
    arow = zt_ref[N_HEADS + h:N_HEADS + h + 1, rs]
    gtot = acol[chunk - 1:chunk, :]
    m_prev = m_ref[h:h + 1, 0:1]
    ct_prev = ct_ref[h]
    ct_prev_b = ct_prev.astype(BF16)

    q_b = qk_ext[rows, OFF_Q + h * HEAD_DIM:OFF_Q + (h + 1) * HEAD_DIM].astype(BF16)
    k_h = qk_ext[rows, OFF_K + h * HEAD_DIM:OFF_K + (h + 1) * HEAD_DIM]
    k_b = k_h.astype(BF16)
    v_aug = jnp.concatenate([vb_ref[rs, hs], ones_col], axis=1)

    w_row = gtot - arow + irow
    m_loc = jnp.max(w_row, axis=1, keepdims=True)
    e_col = jnp.exp(gtot - acol + icol - m_loc)
    m_new = jnp.maximum(gtot + m_prev, m_loc)
    s_old = jnp.exp(gtot + m_prev - m_new)
    s_new = jnp.exp(m_loc - m_new)
    ke_t = (e_col * k_h).T.astype(BF16)

    dlog = acol - arow + irow + bias_ref[...]
    m_inter = acol + m_prev
    m_t = jnp.maximum(m_inter, jnp.max(dlog, axis=1, keepdims=True))
    p = jnp.exp(dlog - m_t)
    s_inter = jnp.exp(m_inter - m_t)
    yield

    s = lax.dot_general(q_b, k_b, (((1,), (1,)), ((), ())), preferred_element_type=F32)
    ct_ref[h] = s_old * ct_prev + s_new * jnp.dot(ke_t, v_aug, preferred_element_type=F32)
    m_ref[h:h + 1, :] = jnp.broadcast_to(m_new, (1, LANES))
    yield

    sp_b = (s * p).astype(BF16)
    tot = (jnp.dot(sp_b, v_aug, preferred_element_type=F32)
           + s_inter * jnp.dot(q_b, ct_prev_b, preferred_element_type=F32))
    yield

    num = tot[:, :HEAD_DIM]
    den = tot[:, HEAD_DIM:HEAD_DIM + 1]
    hh = num / jnp.maximum(jnp.abs(den), jnp.exp(-m_t))
    mu = jnp.mean(hh, axis=1, keepdims=True)
    hc = hh - mu
    var = jnp.mean(hc * hc, axis=1, keepdims=True)
    hn = hc * lax.rsqrt(var + LN_EPS) * hng_ref[:, hs]
    ycat[rs, hs] = (jax.nn.sigmoid(o_buf[rs, hs]) * hn).astype(BF16)
    yield


HEAD_PHASE_ORDER = (0, 1, "T", 2, 0, "T", 3, 1, 0, "T", 2, 1, 0, "T", 3, 2, 1, "T", 3, 2, 3)
HEAD_PHASE_TICKS = (("D", "U", "D", "-", "-"), ("U", "U", "U", "U", "U"))


def _mixer_stage(mlp_down, mlp_up, x_ref, t_glob, win_ref, bg_ref, wconv_ref, hng_ref,
                 wpool_ref, pscale_ref, wout_ref, tri_ref, bias_ref, z1_buf, xb_buf,
                 qk_ext, p_ext, vb_ref, o_buf, z_ref, zt_ref, ycat, ct_ref, m_ref,
                 *, ts, chunk):
    n_chunks = ts // chunk
    xb_buf[...] = x_ref[...].astype(BF16)
    wc = wconv_ref[...]

    def project(lo, width):
        return jnp.dot(xb_buf[...], win_ref[:, lo:lo + width], preferred_element_type=F32)

    def conv_group(g):
        cols = slice(g * MXU_DIM, (g + 1) * MXU_DIM)
        scale = 1.0 if g * MXU_DIM < OFF_K else HEAD_DIM ** -0.5
        tail = qk_ext[ts:ts + CONV_HALO, cols]
        for c in reversed(range(n_chunks)):
            blk = qk_ext[c * chunk:c * chunk + chunk + CONV_HALO, cols]
            y = wc[0:1, cols] * pltpu.roll(blk, CONV_WIDTH - 1, 0)[CONV_HALO:, :]
            for j in range(1, CONV_WIDTH - 1):
                y = y + wc[j:j + 1, cols] * pltpu.roll(blk, CONV_WIDTH - 1 - j, 0)[CONV_HALO:, :]
            y = y + wc[CONV_WIDTH - 1:CONV_WIDTH, cols] * blk[CONV_HALO:, :]
            act = jax.nn.silu(y)
            rows = slice(CONV_HALO + c * chunk, CONV_HALO + (c + 1) * chunk)
            qk_ext[rows, cols] = act if scale == 1.0 else act * scale
        qk_ext[0:CONV_HALO, cols] = tail

    mlp_down(1)
    qk_ext[CONV_HALO:CONV_HALO + ts, :] = project(OFF_Q, OFF_V - OFF_Q)
    mlp_up(1)

    n_qk = (OFF_V - OFF_Q) // MXU_DIM
    conv_group(0)
    vb_ref[...] = project(OFF_V, D_MLSTM).astype(BF16)
    conv_group(1)
    mlp_down(1)
    o_buf[...] = project(OFF_O, D_MLSTM)
    conv_group(2)
    mlp_down(1)
    u_pg = project(OFF_P, D_IN_PAD - OFF_P)
    p_ext[POOL_HALO:POOL_HALO + ts, :] = u_pg[:, :D_POOL]
    conv_group(3)
    gates = u_pg[:, D_POOL:] + bg_ref[...]
    logf_parts = _split3_bf16(jax.nn.log_sigmoid(gates))
    lane = lax.broadcasted_iota(jnp.int32, (chunk, LANES), 1)
    for c in range(n_chunks):
        rs = slice(c * chunk, (c + 1) * chunk)
        a_c = sum(jnp.dot(tri_ref[...], part[rs, :], preferred_element_type=F32)
                  for part in logf_parts)
        z_c = jnp.where(lane < N_HEADS, gates[rs, :], a_c)
        z_ref[rs, :] = z_c
        zt_ref[:, rs] = z_c.T

    for c in range(n_chunks):
        heads = [_head_phases(c, h, hng_ref, bias_ref, qk_ext, vb_ref, o_buf, z_ref,
                              zt_ref, ycat, ct_ref, m_ref, chunk=chunk) for h in range(N_HEADS)]
        ticks = iter(HEAD_PHASE_TICKS[c])
        for h in HEAD_PHASE_ORDER:
            if h == "T":
                {"D": mlp_down, "U": mlp_up, "-": lambda n: None}[next(ticks)](1)
            else:
                next(heads[h])

    ext = p_ext[...]
    p_ext[0:POOL_HALO, :] = ext[ts:ts + POOL_HALO, :]
    count = (t_glob * ts + 1 + lax.broadcasted_iota(jnp.int32, (ts, 1), 0)).astype(F32)
    wsum = ext
    shift = 1
    deltas = []
    for gi, win in enumerate(POOL_WINDOWS):
        while shift < win:
            wsum = wsum + pltpu.roll(wsum, shift, 0)
            shift *= 2
        cur = ext[POOL_HALO:, gi * POOL_GROUP_DIM:(gi + 1) * POOL_GROUP_DIM]
        d = wsum[POOL_HALO:, :POOL_GROUP_DIM] / jnp.minimum(count, float(win)) - cur
        deltas.append(d.astype(BF16))
        wsum = wsum[:, POOL_GROUP_DIM:]
    mlp_up(1)
    for gi in range(len(POOL_WINDOWS)):
        ls = slice(gi * POOL_GROUP_DIM, (gi + 1) * POOL_GROUP_DIM)
        yp = jnp.dot(deltas[gi], wpool_ref[gi], preferred_element_type=F32) * pscale_ref[:, ls]
        ycat[:, D_MLSTM + gi * POOL_GROUP_DIM:D_MLSTM + (gi + 1) * POOL_GROUP_DIM] = yp.astype(BF16)
    mlp_down(1)
    mlp_up(2)

    mix = jnp.dot(ycat[...], wout_ref[...], preferred_element_type=F32)
    z1_buf[...] = ALPHA * x_ref[...] + mix


def _layer_kernel(layer_ref, x_ref, win_ref, bg_ref, wconv_ref, hng_ref, wpool_ref, pscale_ref,
                  wout_ref, g1_ref, b1_ref, w1_ref, w2_ref, g2_ref, b2_ref, tri_ref, bias_ref,
                  o_ref, z1_buf, x1_buf, x1b_buf, act_buf, xb_buf, qk_ext, p_ext, vb_ref, o_buf,
                  z_ref, zt_ref, ycat, ct_ref, m_ref, *, ts, chunk, nt):
    i = pl.program_id(0)
    t_glob = lax.rem(i, nt)

    @pl.when(i == 0)
    def _():
        z1_buf[...] = jnp.zeros_like(z1_buf)
        x1_buf[...] = jnp.zeros_like(x1_buf)
        act_buf[...] = jnp.zeros_like(act_buf)

    @pl.when(t_glob == 0)
    def _():
        qk_ext[0:CONV_HALO, :] = jnp.zeros((CONV_HALO, 2 * D_MLSTM), F32)
        p_ext[0:POOL_HALO, :] = jnp.zeros((POOL_HALO, D_POOL), F32)
        ct_ref[...] = jnp.zeros_like(ct_ref)
        m_ref[...] = jnp.zeros_like(m_ref)

    down = _mlp_down_phases(act_buf, x1_buf, w2_ref, g2_ref, b2_ref, o_ref)
    up = _mlp_up_phases(z1_buf, g1_ref, b1_ref, w1_ref, x1_buf, x1b_buf, act_buf)
    exhausted = object()

    def advance(stage, n):
        for _ in range(n):
            assert next(stage, exhausted) is not exhausted

    _mixer_stage(
        functools.partial(advance, down), functools.partial(advance, up),
        x_ref, t_glob, win_ref, bg_ref, wconv_ref, hng_ref, wpool_ref, pscale_ref,
        wout_ref, tri_ref, bias_ref, z1_buf, xb_buf,
        qk_ext, p_ext, vb_ref, o_buf, z_ref, zt_ref, ycat, ct_ref, m_ref, ts=ts, chunk=chunk)
    assert next(down, exhausted) is exhausted and next(up, exhausted) is exhausted


def _layer_spec(shape):
    return pl.BlockSpec((None,) + shape, lambda i, layer: (layer[0],) + (0,) * len(shape),
                        pipeline_mode=pl.Buffered(1))


def _const_spec(shape):
    return pl.BlockSpec(shape, lambda i, layer: (0,) * len(shape),
                        pipeline_mode=pl.Buffered(1))


def _layer_call(x2d, params, consts, layer, *, seq):
    ts = SEQ_TILE
    chunk = MLSTM_CHUNK
    n_tiles = x2d.shape[0] // ts
    shapes = [
        (D_MODEL, D_IN_PAD), (1, LANES), (CONV_WIDTH, 2 * D_MLSTM), (1, D_MLSTM),
        (len(POOL_WINDOWS), POOL_GROUP_DIM, POOL_GROUP_DIM), (1, D_POOL),
        (D_MODEL, D_MODEL), (1, D_MODEL), (1, D_MODEL),
        (D_MODEL, D_FF), (D_FF, D_MODEL), (1, D_MODEL), (1, D_MODEL),
    ]
    x_spec = pl.BlockSpec((ts, D_MODEL), lambda i, layer: (jnp.minimum(i, n_tiles - 1), 0))
    o_spec = pl.BlockSpec((ts, D_MODEL), lambda i, layer: (jnp.maximum(i - 2, 0), 0))
    grid_spec = pltpu.PrefetchScalarGridSpec(
        num_scalar_prefetch=1,
        grid=(n_tiles + 2,),
        in_specs=([x_spec] + [_layer_spec(s) for s in shapes]
                  + [_const_spec((chunk, chunk)), _const_spec((chunk, chunk))]),
        out_specs=o_spec,
        scratch_shapes=[
            pltpu.VMEM((ts, D_MODEL), F32),
            pltpu.VMEM((ts, D_MODEL), F32),
            pltpu.VMEM((ts, D_MODEL), BF16),
            pltpu.VMEM((ts, D_FF), BF16),
            pltpu.VMEM((ts, D_MODEL), BF16),
            pltpu.VMEM((ts + CONV_HALO, 2 * D_MLSTM), F32),
            pltpu.VMEM((ts + POOL_HALO, D_POOL), F32),
            pltpu.VMEM((ts, D_MLSTM), BF16),
            pltpu.VMEM((ts, D_MLSTM), F32),
            pltpu.VMEM((ts, LANES), F32),
            pltpu.VMEM((LANES, ts), F32),
            pltpu.VMEM((ts, D_MODEL), BF16),
            pltpu.VMEM((N_HEADS, HEAD_DIM, 2 * HEAD_DIM), F32),
            pltpu.VMEM((8, LANES), F32),
        ])
    return pl.pallas_call(
        functools.partial(_layer_kernel, ts=ts, chunk=chunk, nt=seq // ts),
        grid_spec=grid_spec,
        out_shape=jax.ShapeDtypeStruct(x2d.shape, F32),
        compiler_params=pltpu.CompilerParams(
            dimension_semantics=("arbitrary",),
            vmem_limit_bytes=VMEM_LIMIT_BYTES),
        name="layer",
    )(jnp.full((1,), layer, jnp.int32), x2d, *params, *consts)


def kernel(x, w_in, b_gate, w_conv, hn_g, w_pool, pool_scale, w_out,
           ln1_g, ln1_b, w_ff1, w_ff2, ln2_g, ln2_b):
    batch, seq, d_model = x.shape
    assert d_model == D_MODEL and seq % SEQ_TILE == 0 and SEQ_TILE % MLSTM_CHUNK == 0
    n_gate = 2 * N_HEADS
    off_gate_src = 3 * D_MLSTM
    win = jnp.concatenate([
        w_in[:, :, :off_gate_src].astype(BF16),
        w_in[:, :, off_gate_src + n_gate:].astype(BF16),
        w_in[:, :, off_gate_src:off_gate_src + n_gate].astype(BF16),
        jnp.zeros((DEPTH, D_MODEL, LANES - n_gate), BF16)], axis=-1)
    params = (
        win,
        jnp.pad(b_gate, ((0, 0), (0, LANES - n_gate)))[:, None, :],
        w_conv, hn_g[:, None, :], w_pool.astype(BF16), pool_scale[:, None, :],
        w_out.astype(BF16), ln1_g[:, None, :], ln1_b[:, None, :],
        w_ff1.astype(BF16), w_ff2.astype(BF16), ln2_g[:, None, :], ln2_b[:, None, :],
    )
    causal = jnp.tril(jnp.ones((MLSTM_CHUNK, MLSTM_CHUNK), dtype=bool))
    consts = (causal.astype(BF16), jnp.where(causal, 0.0, -jnp.inf).astype(F32))
    h = x.reshape(batch * seq, d_model)
    for layer in range(DEPTH):
        h = _layer_call(h, params, consts, layer, seq=seq)
    return h.reshape(batch, seq, d_model)
```

```python
import functools

import jax
import jax.numpy as jnp
from jax import lax
from jax.experimental import pallas as pl
from jax.experimental.pallas import tpu as pltpu

D_MODEL = 1024
DEPTH = 4
N_HEADS = 4
HEAD_DIM = 128
D_MLSTM = N_HEADS * HEAD_DIM
POOL_WINDOWS = (2, 4, 8, 16)
POOL_GROUP_DIM = 128
D_POOL = len(POOL_WINDOWS) * POOL_GROUP_DIM
CONV_WIDTH = 4
D_FF = 4 * D_MODEL
ALPHA = (2.0 * DEPTH) ** 0.25
LN_EPS = 1e-5

OFF_Q = 0
OFF_K = OFF_Q + D_MLSTM
OFF_V = OFF_K + D_MLSTM
OFF_O = OFF_V + D_MLSTM
OFF_P = OFF_O + D_MLSTM
OFF_G = OFF_P + D_POOL
LANES = 128
D_IN_PAD = OFF_G + LANES

CONV_HALO = 8
POOL_HALO = 16

SEQ_TILE = 512
MLSTM_CHUNK = 256
FF_CHUNK = 512
VMEM_LIMIT_BYTES = 60000 * 1024

F32 = jnp.float32
BF16 = jnp.bfloat16


def _layer_norm_rows(z, g, b):
    mu = jnp.mean(z, axis=-1, keepdims=True)
    zc = z - mu
    var = jnp.mean(zc * zc, axis=-1, keepdims=True)
    return zc * lax.rsqrt(var + LN_EPS) * g + b


def _cumsum_rows(tri, x):
    p1 = x.astype(BF16)
    r1 = x - p1.astype(F32)
    p2 = r1.astype(BF16)
    p3 = (r1 - p2.astype(F32)).astype(BF16)
    out = jnp.dot(tri, p1, preferred_element_type=F32)
    out = out + jnp.dot(tri, p2, preferred_element_type=F32)
    return out + jnp.dot(tri, p3, preferred_element_type=F32)


def _mlp_phases(x1_buf, w1_ref, w2_ref, g2_ref, b2_ref, o_ref):
    x1 = x1_buf[...]
    xb = x1.astype(BF16)
    acc = None
    for c in range(D_FF // FF_CHUNK):
        cs = slice(c * FF_CHUNK, (c + 1) * FF_CHUNK)
        hid = jnp.dot(xb, w1_ref[:, cs], preferred_element_type=F32)
        yield
        act = jnp.square(jnp.maximum(hid, 0.0)).astype(BF16)
        part = jnp.dot(act, w2_ref[cs, :], preferred_element_type=F32)
        acc = part if acc is None else acc + part
        yield
    o_ref[...] = _layer_norm_rows(ALPHA * x1 + acc, g2_ref[...], b2_ref[...])
    yield


def _mixer_phases(x_ref, t_glob, win_ref, bg_ref, wconv_ref, hng_ref, wpool_ref, pscale_ref,
                  wout_ref, g1_ref, b1_ref, tri_ref, bias_ref, x1_buf,
                  qk_ext, p_ext, vb_ref, o_buf, z_ref, zt_ref, ycat, ct_ref, m_ref,
                  *, ts, chunk):
    n_chunks = ts // chunk
    xb = x_ref[...].astype(BF16)

    qk_ext[CONV_HALO:CONV_HALO + ts, :] = jnp.dot(
        xb, win_ref[:, OFF_Q:OFF_V], preferred_element_type=F32)
    yield
    u_vo = jnp.dot(xb, win_ref[:, OFF_V:OFF_P], preferred_element_type=F32)
    vb_ref[...] = u_vo[:, :D_MLSTM].astype(BF16)
    o_buf[...] = u_vo[:, D_MLSTM:]
    yield
    u_pg = jnp.dot(xb, win_ref[:, OFF_P:D_IN_PAD], preferred_element_type=F32)
    p_ext[POOL_HALO:POOL_HALO + ts, :] = u_pg[:, :D_POOL]
    gates = u_pg[:, D_POOL:] + bg_ref[...]
    logf = jax.nn.log_sigmoid(gates)
    lane = lax.broadcasted_iota(jnp.int32, (chunk, LANES), 1)
    for c in range(n_chunks):
        rs = slice(c * chunk, (c + 1) * chunk)
        a_c = _cumsum_rows(tri_ref[...], logf[rs, :])
        z_c = jnp.where(lane < N_HEADS, gates[rs, :], a_c)
        z_ref[rs, :] = z_c
        zt_ref[:, rs] = z_c.T
    yield

    wc = wconv_ref[...]
    conv_tail = qk_ext[ts:ts + CONV_HALO, :]
    for c in reversed(range(n_chunks)):
        blk = qk_ext[c * chunk:c * chunk + chunk + CONV_HALO, :]
        y = wc[0:1, :] * pltpu.roll(blk, CONV_WIDTH - 1, 0)[CONV_HALO:, :]
        for j in range(1, CONV_WIDTH - 1):
            y = y + wc[j:j + 1, :] * pltpu.roll(blk, CONV_WIDTH - 1 - j, 0)[CONV_HALO:, :]
        y = y + wc[CONV_WIDTH - 1:CONV_WIDTH, :] * blk[CONV_HALO:, :]
        act = jax.nn.silu(y)
        rows = slice(CONV_HALO + c * chunk, CONV_HALO + (c + 1) * chunk)
        qk_ext[rows, OFF_Q:OFF_K] = act[:, OFF_Q:OFF_K]
        qk_ext[rows, OFF_K:OFF_V] = act[:, OFF_K:OFF_V] * (HEAD_DIM ** -0.5)
        yield
    qk_ext[0:CONV_HALO, :] = conv_tail

    ones_col = jnp.where(lane == 0, 1.0, 0.0).astype(BF16)
    for c in range(n_chunks):
        rs = slice(c * chunk, (c + 1) * chunk)
        rows = slice(CONV_HALO + c * chunk, CONV_HALO + (c + 1) * chunk)
        for h in range(N_HEADS):
            hs = slice(h * HEAD_DIM, (h + 1) * HEAD_DIM)
            icol = z_ref[rs, h:h + 1]
            acol = z_ref[rs, N_HEADS + h:N_HEADS + h + 1]
            irow = zt_ref[h:h + 1, rs]
            arow = zt_ref[N_HEADS + h:N_HEADS + h + 1, rs]
            gtot = acol[chunk - 1:chunk, :]
            m_prev = m_ref[h:h + 1, 0:1]
            ct_prev = ct_ref[h]

            q_b = qk_ext[rows, OFF_Q + h * HEAD_DIM:OFF_Q + (h + 1) * HEAD_DIM].astype(BF16)
            k_h = qk_ext[rows, OFF_K + h * HEAD_DIM:OFF_K + (h + 1) * HEAD_DIM]
            k_b = k_h.astype(BF16)
            v_aug = jnp.concatenate([vb_ref[rs, hs], ones_col], axis=1)

            w_row = gtot - arow + irow
            m_loc = jnp.max(w_row, axis=1, keepdims=True)
            e_col = jnp.exp(gtot - acol + icol - m_loc)

            dlog = acol - arow + irow + bias_ref[...]
            m_inter = acol + m_prev
            m_t = jnp.maximum(m_inter, jnp.max(dlog, axis=1, keepdims=True))
            p = jnp.exp(dlog - m_t)
            s_inter = jnp.exp(m_inter - m_t)
            s = lax.dot_general(q_b, k_b, (((1,), (1,)), ((), ())), preferred_element_type=F32)
            sp_b = (s * p).astype(BF16)
            tot = (jnp.dot(sp_b, v_aug, preferred_element_type=F32)
                   + s_inter * jnp.dot(q_b, ct_prev.astype(BF16), preferred_element_type=F32))
            num = tot[:, :HEAD_DIM]
            den = tot[:, HEAD_DIM:HEAD_DIM + 1]
            hh = num / jnp.maximum(jnp.abs(den), jnp.exp(-m_t))

            mu = jnp.mean(hh, axis=1, keepdims=True)
            hc = hh - mu
            var = jnp.mean(hc * hc, axis=1, keepdims=True)
            hn = hc * lax.rsqrt(var + LN_EPS) * hng_ref[:, hs]
            ycat[rs, hs] = (jax.nn.sigmoid(o_buf[rs, hs]) * hn).astype(BF16)

            m_new = jnp.maximum(gtot + m_prev, m_loc)
            s_old = jnp.exp(gtot + m_prev - m_new)
            s_new = jnp.exp(m_loc - m_new)
            ke_t = (e_col * k_h).T.astype(BF16)
            ct_ref[h] = s_old * ct_prev + s_new * jnp.dot(ke_t, v_aug, preferred_element_type=F32)
            m_ref[h:h + 1, :] = jnp.broadcast_to(m_new, (1, LANES))
            yield

    ext = p_ext[...]
    p_ext[0:POOL_HALO, :] = ext[ts:ts + POOL_HALO, :]
    count = (t_glob * ts + 1 + lax.broadcasted_iota(jnp.int32, (ts, 1), 0)).astype(F32)
    wsum = ext
    shift = 1
    for gi, win in enumerate(POOL_WINDOWS):
        while shift < win:
            wsum = wsum + pltpu.roll(wsum, shift, 0)
            shift *= 2
        ls = slice(gi * POOL_GROUP_DIM, (gi + 1) * POOL_GROUP_DIM)
        cur = ext[POOL_HALO:, ls]
        d = wsum[POOL_HALO:, :POOL_GROUP_DIM] / jnp.minimum(count, float(win)) - cur
        yp = jnp.dot(d.astype(BF16), wpool_ref[gi], preferred_element_type=F32) * pscale_ref[:, ls]
        ycat[:, D_MLSTM + gi * POOL_GROUP_DIM:D_MLSTM + (gi + 1) * POOL_GROUP_DIM] = yp.astype(BF16)
        wsum = wsum[:, POOL_GROUP_DIM:]
        yield

    mix = jnp.dot(ycat[...], wout_ref[...], preferred_element_type=F32)
    x1_buf[...] = _layer_norm_rows(ALPHA * x_ref[...] + mix, g1_ref[...], b1_ref[...])
    yield


def _layer_kernel(layer_ref, x_ref, win_ref, bg_ref, wconv_ref, hng_ref, wpool_ref, pscale_ref,
                  g1_ref, b1_ref, g2_ref, b2_ref, wout_ref, w1_ref, w2_ref, tri_ref, bias_ref,
                  nwout_ref, nw1_ref, nw2_ref,
                  o_ref, nwout_b_ref, nw1_b_ref, nw2_b_ref,
                  x1_buf, qk_ext, p_ext, vb_ref, o_buf, z_ref, zt_ref, ycat, ct_ref, m_ref,
                  *, ts, chunk, nt):
    i = pl.program_id(0)
    t_glob = lax.rem(i, nt)

    @pl.when(i == 0)
    def _():
        x1_buf[...] = jnp.zeros_like(x1_buf)

    @pl.when(t_glob == 0)
    def _():
        qk_ext[0:CONV_HALO, :] = jnp.zeros((CONV_HALO, 2 * D_MLSTM), F32)
        p_ext[0:POOL_HALO, :] = jnp.zeros((POOL_HALO, D_POOL), F32)
        ct_ref[...] = jnp.zeros_like(ct_ref)
        m_ref[...] = jnp.zeros_like(m_ref)

    nwout_b_ref[...] = nwout_ref[...].astype(BF16)
    nw1_b_ref[...] = nw1_ref[...].astype(BF16)
    nw2_b_ref[...] = nw2_ref[...].astype(BF16)

    mixer = _mixer_phases(
        x_ref, t_glob, win_ref, bg_ref, wconv_ref, hng_ref, wpool_ref, pscale_ref,
        wout_ref, g1_ref, b1_ref, tri_ref, bias_ref, x1_buf,
        qk_ext, p_ext, vb_ref, o_buf, z_ref, zt_ref, ycat, ct_ref, m_ref, ts=ts, chunk=chunk)
    mlp = _mlp_phases(x1_buf, w1_ref, w2_ref, g2_ref, b2_ref, o_ref)
    live = [mlp, mixer]
    while live:
        for gen in list(live):
            try:
                next(gen)
            except StopIteration:
                live.remove(gen)


def _stacked_spec(shape):
    return pl.BlockSpec((None,) + shape, lambda i, layer: (layer[0],) + (0,) * len(shape),
                        pipeline_mode=pl.Buffered(1))


def _whole_spec(shape):
    return pl.BlockSpec(shape, lambda i, layer: (0,) * len(shape),
                        pipeline_mode=pl.Buffered(1))


def _layer_call(x2d, stacked, wout_b, w1_b, w2_b, consts, next_f32, layer, *, seq):
    ts = SEQ_TILE
    chunk = MLSTM_CHUNK
    n_tiles = x2d.shape[0] // ts
    stacked_shapes = [
        (D_MODEL, D_IN_PAD), (1, LANES), (CONV_WIDTH, 2 * D_MLSTM), (1, D_MLSTM),
        (len(POOL_WINDOWS), POOL_GROUP_DIM, POOL_GROUP_DIM), (1, D_POOL),
        (1, D_MODEL), (1, D_MODEL), (1, D_MODEL), (1, D_MODEL),
    ]
    x_spec = pl.BlockSpec((ts, D_MODEL), lambda i, layer: (jnp.minimum(i, n_tiles - 1), 0))
    o_spec = pl.BlockSpec((ts, D_MODEL), lambda i, layer: (jnp.maximum(i - 1, 0), 0))

    def slab_in(rows, cols):
        return pl.BlockSpec(
            (None, rows // n_tiles, cols),
            lambda i, layer: (jnp.minimum(layer[0] + 1, DEPTH - 1), jnp.minimum(i, n_tiles - 1), 0))

    def slab_out(rows, cols):
        return pl.BlockSpec((rows // n_tiles, cols),
                            lambda i, layer: (jnp.minimum(i, n_tiles - 1), 0))

    big = [(D_MODEL, D_MODEL), (D_MODEL, D_FF), (D_FF, D_MODEL)]
    grid_spec = pltpu.PrefetchScalarGridSpec(
        num_scalar_prefetch=1,
        grid=(n_tiles + 1,),
        in_specs=([x_spec] + [_stacked_spec(s) for s in stacked_shapes]
                  + [_whole_spec(s) for s in big]
                  + [_whole_spec((chunk, chunk)), _whole_spec((chunk, chunk))]
                  + [slab_in(*s) for s in big]),
        out_specs=[o_spec] + [slab_out(*s) for s in big],
        scratch_shapes=[
            pltpu.VMEM((ts, D_MODEL), F32),
            pltpu.VMEM((ts + CONV_HALO, 2 * D_MLSTM), F32),
            pltpu.VMEM((ts + POOL_HALO, D_POOL), F32),
            pltpu.VMEM((ts, D_MLSTM), BF16),
            pltpu.VMEM((ts, D_MLSTM), F32),
            pltpu.VMEM((ts, LANES), F32),
            pltpu.VMEM((LANES, ts), F32),
            pltpu.VMEM((ts, D_MODEL), BF16),
            pltpu.VMEM((N_HEADS, HEAD_DIM, 2 * HEAD_DIM), F32),
            pltpu.VMEM((8, LANES), F32),
        ])
    return pl.pallas_call(
        functools.partial(_layer_kernel, ts=ts, chunk=chunk, nt=seq // ts),
        grid_spec=grid_spec,
        out_shape=[jax.ShapeDtypeStruct(x2d.shape, F32)]
        + [jax.ShapeDtypeStruct(s, BF16) for s in big],
        compiler_params=pltpu.CompilerParams(
            dimension_semantics=("arbitrary",),
            vmem_limit_bytes=VMEM_LIMIT_BYTES),
        name="layer",
    )(jnp.full((1,), layer, jnp.int32), x2d, *stacked, wout_b, w1_b, w2_b, *consts, *next_f32)


def kernel(x, w_in, b_gate, w_conv, hn_g, w_pool, pool_scale, w_out,
           ln1_g, ln1_b, w_ff1, w_ff2, ln2_g, ln2_b):
    batch, seq, d_model = x.shape
    assert d_model == D_MODEL and seq % SEQ_TILE == 0 and SEQ_TILE % MLSTM_CHUNK == 0
    n_gate = 2 * N_HEADS
    off_gate_src = 3 * D_MLSTM
    win = jnp.concatenate([
        w_in[:, :, :off_gate_src].astype(BF16),
        w_in[:, :, off_gate_src + n_gate:].astype(BF16),
        w_in[:, :, off_gate_src:off_gate_src + n_gate].astype(BF16),
        jnp.zeros((DEPTH, D_MODEL, LANES - n_gate), BF16)], axis=-1)
    stacked = (
        win,
        jnp.pad(b_gate, ((0, 0), (0, LANES - n_gate)))[:, None, :],
        w_conv, hn_g[:, None, :], w_pool.astype(BF16), pool_scale[:, None, :],
        ln1_g[:, None, :], ln1_b[:, None, :], ln2_g[:, None, :], ln2_b[:, None, :],
    )
    causal = jnp.tril(jnp.ones((MLSTM_CHUNK, MLSTM_CHUNK), dtype=bool))
    consts = (causal.astype(BF16), jnp.where(causal, 0.0, -jnp.inf).astype(F32))
    next_f32 = (w_out, w_ff1, w_ff2)
    big_b = (w_out[0].astype(BF16), w_ff1[0].astype(BF16), w_ff2[0].astype(BF16))
    h = x.reshape(batch * seq, d_model)
    for layer in range(DEPTH):
        h, *big_b = _layer_call(h, stacked, *big_b, consts, next_f32, layer, seq=seq)
    return h.reshape(batch, seq, d_model)
```

```python
import functools

import jax
import jax.numpy as jnp
from jax import lax
from jax.experimental import pallas as pl
from jax.experimental.pallas import tpu as pltpu

D_MODEL = 1024
DEPTH = 4
N_HEADS = 4
HEAD_DIM = 128
D_MLSTM = N_HEADS * HEAD_DIM
POOL_WINDOWS = (2, 4, 8, 16)
POOL_GROUP_DIM = 128
D_POOL = len(POOL_WINDOWS) * POOL_GROUP_DIM
CONV_WIDTH = 4
D_FF = 4 * D_MODEL
ALPHA = (2.0 * DEPTH) ** 0.25
LN_EPS = 1e-5

OFF_Q = 0
OFF_K = OFF_Q + D_MLSTM
OFF_V = OFF_K + D_MLSTM
OFF_O = OFF_V + D_MLSTM
OFF_P = OFF_O + D_MLSTM
OFF_G = OFF_P + D_POOL
LANES = 128
D_IN_PAD = OFF_G + LANES

CONV_HALO = 8
POOL_HALO = 16

SEQ_TILE = 512
MLSTM_CHUNK = 256
FF_CHUNK = 512
VMEM_LIMIT_BYTES = 60000 * 1024

F32 = jnp.float32
BF16 = jnp.bfloat16

PHASE_ORDER = "XMM" + "XM" * 2 + "XMM" * 2 + "XM" * 8 + "XXM" + "XXX" + "M"


def _layer_norm_rows(z, g, b):
    mu = jnp.mean(z, axis=-1, keepdims=True)
    zc = z - mu
    var = jnp.mean(zc * zc, axis=-1, keepdims=True)
    return zc * lax.rsqrt(var + LN_EPS) * g + b


def _cumsum_rows(tri, x):
    p1 = x.astype(BF16)
    r1 = x - p1.astype(F32)
    p2 = r1.astype(BF16)
    p3 = (r1 - p2.astype(F32)).astype(BF16)
    out = jnp.dot(tri, p1, preferred_element_type=F32)
    out = out + jnp.dot(tri, p2, preferred_element_type=F32)
    return out + jnp.dot(tri, p3, preferred_element_type=F32)


def _mlp_phases(z1_buf, g1_ref, b1_ref, w1_ref, w2_ref, g2_ref, b2_ref, o_ref):
    x1 = _layer_norm_rows(z1_buf[...], g1_ref[...], b1_ref[...])
    xb = x1.astype(BF16)
    yield
    acc = None
    for c in range(D_FF // FF_CHUNK):
        cs = slice(c * FF_CHUNK, (c + 1) * FF_CHUNK)
        hid = jnp.dot(xb, w1_ref[:, cs], preferred_element_type=F32)
        yield
        act = jnp.square(jnp.maximum(hid, 0.0)).astype(BF16)
        part = jnp.dot(act, w2_ref[cs, :], preferred_element_type=F32)
        acc = part if acc is None else acc + part
        yield
    o_ref[...] = _layer_norm_rows(ALPHA * x1 + acc, g2_ref[...], b2_ref[...])
    yield


def _mixer_phases(x_ref, t_glob, win_ref, bg_ref, wconv_ref, hng_ref, wpool_ref, pscale_ref,
                  wout_ref, tri_ref, bias_ref, z1_buf,
                  qk_ext, p_ext, vb_ref, o_buf, z_ref, zt_ref, ycat, ct_ref, m_ref,
                  *, ts, chunk):
    n_chunks = ts // chunk
    xb = x_ref[...].astype(BF16)

    qk_ext[CONV_HALO:CONV_HALO + ts, :] = jnp.dot(
        xb, win_ref[:, OFF_Q:OFF_V], preferred_element_type=F32)
    yield
    u_vo = jnp.dot(xb, win_ref[:, OFF_V:OFF_P], preferred_element_type=F32)
    vb_ref[...] = u_vo[:, :D_MLSTM].astype(BF16)
    o_buf[...] = u_vo[:, D_MLSTM:]
    yield
    u_pg = jnp.dot(xb, win_ref[:, OFF_P:D_IN_PAD], preferred_element_type=F32)
    p_ext[POOL_HALO:POOL_HALO + ts, :] = u_pg[:, :D_POOL]
    gates = u_pg[:, D_POOL:] + bg_ref[...]
    logf = jax.nn.log_sigmoid(gates)
    lane = lax.broadcasted_iota(jnp.int32, (chunk, LANES), 1)
    for c in range(n_chunks):
        rs = slice(c * chunk, (c + 1) * chunk)
        a_c = _cumsum_rows(tri_ref[...], logf[rs, :])
        z_c = jnp.where(lane < N_HEADS, gates[rs, :], a_c)
        z_ref[rs, :] = z_c
        zt_ref[:, rs] = z_c.T
    yield

    wc = wconv_ref[...]
    conv_tail = qk_ext[ts:ts + CONV_HALO, :]
    for c in reversed(range(n_chunks)):
        blk = qk_ext[c * chunk:c * chunk + chunk + CONV_HALO, :]
        y = wc[0:1, :] * pltpu.roll(blk, CONV_WIDTH - 1, 0)[CONV_HALO:, :]
        for j in range(1, CONV_WIDTH - 1):
            y = y + wc[j:j + 1, :] * pltpu.roll(blk, CONV_WIDTH - 1 - j, 0)[CONV_HALO:, :]
        y = y + wc[CONV_WIDTH - 1:CONV_WIDTH, :] * blk[CONV_HALO:, :]
        act = jax.nn.silu(y)
        rows = slice(CONV_HALO + c * chunk, CONV_HALO + (c + 1) * chunk)
        qk_ext[rows, OFF_Q:OFF_K] = act[:, OFF_Q:OFF_K]
        qk_ext[rows, OFF_K:OFF_V] = act[:, OFF_K:OFF_V] * (HEAD_DIM ** -0.5)
        yield
    qk_ext[0:CONV_HALO, :] = conv_tail

    ones_col = jnp.where(lane == 0, 1.0, 0.0).astype(BF16)
    for c in range(n_chunks):
        rs = slice(c * chunk, (c + 1) * chunk)
        rows = slice(CONV_HALO + c * chunk, CONV_HALO + (c + 1) * chunk)
        for h in range(N_HEADS):
            hs = slice(h * HEAD_DIM, (h + 1) * HEAD_DIM)
            icol = z_ref[rs, h:h + 1]
            acol = z_ref[rs, N_HEADS + h:N_HEADS + h + 1]
            irow = zt_ref[h:h + 1, rs]
            arow = zt_ref[N_HEADS + h:N_HEADS + h + 1, rs]
            gtot = acol[chunk - 1:chunk, :]
            m_prev = m_ref[h:h + 1, 0:1]
            ct_prev = ct_ref[h]

            q_b = qk_ext[rows, OFF_Q + h * HEAD_DIM:OFF_Q + (h + 1) * HEAD_DIM].astype(BF16)
            k_h = qk_ext[rows, OFF_K + h * HEAD_DIM:OFF_K + (h + 1) * HEAD_DIM]
            k_b = k_h.astype(BF16)
            v_aug = jnp.concatenate([vb_ref[rs, hs], ones_col], axis=1)

            w_row = gtot - arow + irow
            m_loc = jnp.max(w_row, axis=1, keepdims=True)
            e_col = jnp.exp(gtot - acol + icol - m_loc)

            dlog = acol - arow + irow + bias_ref[...]
            m_inter = acol + m_prev
            m_t = jnp.maximum(m_inter, jnp.max(dlog, axis=1, keepdims=True))
            p = jnp.exp(dlog - m_t)
            s_inter = jnp.exp(m_inter - m_t)
            s = lax.dot_general(q_b, k_b, (((1,), (1,)), ((), ())), preferred_element_type=F32)
            sp_b = (s * p).astype(BF16)
            tot = (jnp.dot(sp_b, v_aug, preferred_element_type=F32)
                   + s_inter * jnp.dot(q_b, ct_prev.astype(BF16), preferred_element_type=F32))
            num = tot[:, :HEAD_DIM]
            den = tot[:, HEAD_DIM:HEAD_DIM + 1]
            hh = num / jnp.maximum(jnp.abs(den), jnp.exp(-m_t))

            mu = jnp.mean(hh, axis=1, keepdims=True)
            hc = hh - mu
            var = jnp.mean(hc * hc, axis=1, keepdims=True)
            hn = hc * lax.rsqrt(var + LN_EPS) * hng_ref[:, hs]
            ycat[rs, hs] = (jax.nn.sigmoid(o_buf[rs, hs]) * hn).astype(BF16)

            m_new = jnp.maximum(gtot + m_prev, m_loc)
            s_old = jnp.exp(gtot + m_prev - m_new)
            s_new = jnp.exp(m_loc - m_new)
            ke_t = (e_col * k_h).T.astype(BF16)
            ct_ref[h] = s_old * ct_prev + s_new * jnp.dot(ke_t, v_aug, preferred_element_type=F32)
            m_ref[h:h + 1, :] = jnp.broadcast_to(m_new, (1, LANES))
            yield

    ext = p_ext[...]
    p_ext[0:POOL_HALO, :] = ext[ts:ts + POOL_HALO, :]
    count = (t_glob * ts + 1 + lax.broadcasted_iota(jnp.int32, (ts, 1), 0)).astype(F32)
    wsum = ext
    shift = 1
    for gi, win in enumerate(POOL_WINDOWS):
        while shift < win:
            wsum = wsum + pltpu.roll(wsum, shift, 0)
            shift *= 2
        ls = slice(gi * POOL_GROUP_DIM, (gi + 1) * POOL_GROUP_DIM)
        cur = ext[POOL_HALO:, ls]
        d = wsum[POOL_HALO:, :POOL_GROUP_DIM] / jnp.minimum(count, float(win)) - cur
        yp = jnp.dot(d.astype(BF16), wpool_ref[gi], preferred_element_type=F32) * pscale_ref[:, ls]
        ycat[:, D_MLSTM + gi * POOL_GROUP_DIM:D_MLSTM + (gi + 1) * POOL_GROUP_DIM] = yp.astype(BF16)
        wsum = wsum[:, POOL_GROUP_DIM:]
        yield

    mix = jnp.dot(ycat[...], wout_ref[...], preferred_element_type=F32)
    z1_buf[...] = ALPHA * x_ref[...] + mix
    yield


def _layer_kernel(layer_ref, x_ref, win_ref, bg_ref, wconv_ref, hng_ref, wpool_ref, pscale_ref,
                  g1_ref, b1_ref, g2_ref, b2_ref, wout_ref, w1_ref, w2_ref, tri_ref, bias_ref,
                  nwout_ref, nw1_ref, nw2_ref,
                  o_ref, nwout_b_ref, nw1_b_ref, nw2_b_ref,
                  z1_buf, qk_ext, p_ext, vb_ref, o_buf, z_ref, zt_ref, ycat, ct_ref, m_ref,
                  *, ts, chunk, nt):
    i = pl.program_id(0)
    t_glob = lax.rem(i, nt)

    @pl.when(i == 0)
    def _():
        z1_buf[...] = jnp.zeros_like(z1_buf)

    @pl.when(t_glob == 0)
    def _():
        qk_ext[0:CONV_HALO, :] = jnp.zeros((CONV_HALO, 2 * D_MLSTM), F32)
        p_ext[0:POOL_HALO, :] = jnp.zeros((POOL_HALO, D_POOL), F32)
        ct_ref[...] = jnp.zeros_like(ct_ref)
        m_ref[...] = jnp.zeros_like(m_ref)

    nwout_b_ref[...] = nwout_ref[...].astype(BF16)
    nw1_b_ref[...] = nw1_ref[...].astype(BF16)
    nw2_b_ref[...] = nw2_ref[...].astype(BF16)

    mixer = _mixer_phases(
        x_ref, t_glob, win_ref, bg_ref, wconv_ref, hng_ref, wpool_ref, pscale_ref,
        wout_ref, tri_ref, bias_ref, z1_buf,
        qk_ext, p_ext, vb_ref, o_buf, z_ref, zt_ref, ycat, ct_ref, m_ref, ts=ts, chunk=chunk)
    mlp = _mlp_phases(z1_buf, g1_ref, b1_ref, w1_ref, w2_ref, g2_ref, b2_ref, o_ref)
    stages = {"X": mixer, "M": mlp}
    for tag in PHASE_ORDER:
        next(stages[tag])
    done = object()
    for gen in stages.values():
        assert next(gen, done) is done


def _stacked_spec(shape):
    return pl.BlockSpec((None,) + shape, lambda i, layer: (layer[0],) + (0,) * len(shape),
                        pipeline_mode=pl.Buffered(1))


def _whole_spec(shape):
    return pl.BlockSpec(shape, lambda i, layer: (0,) * len(shape),
                        pipeline_mode=pl.Buffered(1))


def _layer_call(x2d, stacked, wout_b, w1_b, w2_b, consts, next_f32, layer, *, seq):
    ts = SEQ_TILE
    chunk = MLSTM_CHUNK
    n_tiles = x2d.shape[0] // ts
    stacked_shapes = [
        (D_MODEL, D_IN_PAD), (1, LANES), (CONV_WIDTH, 2 * D_MLSTM), (1, D_MLSTM),
        (len(POOL_WINDOWS), POOL_GROUP_DIM, POOL_GROUP_DIM), (1, D_POOL),
        (1, D_MODEL), (1, D_MODEL), (1, D_MODEL), (1, D_MODEL),
    ]
    x_spec = pl.BlockSpec((ts, D_MODEL), lambda i, layer: (jnp.minimum(i, n_tiles - 1), 0))
    o_spec = pl.BlockSpec((ts, D_MODEL), lambda i, layer: (jnp.maximum(i - 1, 0), 0))

    def slab_in(rows, cols):
        return pl.BlockSpec(
            (None, rows // n_tiles, cols),
            lambda i, layer: (jnp.minimum(layer[0] + 1, DEPTH - 1), jnp.minimum(i, n_tiles - 1), 0))

    def slab_out(rows, cols):
        return pl.BlockSpec((rows // n_tiles, cols),
                            lambda i, layer: (jnp.minimum(i, n_tiles - 1), 0))

    big = [(D_MODEL, D_MODEL), (D_MODEL, D_FF), (D_FF, D_MODEL)]
    grid_spec = pltpu.PrefetchScalarGridSpec(
        num_scalar_prefetch=1,
        grid=(n_tiles + 1,),
        in_specs=([x_spec] + [_stacked_spec(s) for s in stacked_shapes]
                  + [_whole_spec(s) for s in big]
                  + [_whole_spec((chunk, chunk)), _whole_spec((chunk, chunk))]
                  + [slab_in(*s) for s in big]),
        out_specs=[o_spec] + [slab_out(*s) for s in big],
        scratch_shapes=[
            pltpu.VMEM((ts, D_MODEL), F32),
            pltpu.VMEM((ts + CONV_HALO, 2 * D_MLSTM), F32),
            pltpu.VMEM((ts + POOL_HALO, D_POOL), F32),
            pltpu.VMEM((ts, D_MLSTM), BF16),
            pltpu.VMEM((ts, D_MLSTM), F32),
            pltpu.VMEM((ts, LANES), F32),
            pltpu.VMEM((LANES, ts), F32),
            pltpu.VMEM((ts, D_MODEL), BF16),
            pltpu.VMEM((N_HEADS, HEAD_DIM, 2 * HEAD_DIM), F32),
            pltpu.VMEM((8, LANES), F32),
        ])
    return pl.pallas_call(
        functools.partial(_layer_kernel, ts=ts, chunk=chunk, nt=seq // ts),
        grid_spec=grid_spec,
        out_shape=[jax.ShapeDtypeStruct(x2d.shape, F32)]
        + [jax.ShapeDtypeStruct(s, BF16) for s in big],
        compiler_params=pltpu.CompilerParams(
            dimension_semantics=("arbitrary",),
            vmem_limit_bytes=VMEM_LIMIT_BYTES),
        name="layer",
    )(jnp.full((1,), layer, jnp.int32), x2d, *stacked, wout_b, w1_b, w2_b, *consts, *next_f32)


def kernel(x, w_in, b_gate, w_conv, hn_g, w_pool, pool_scale, w_out,
           ln1_g, ln1_b, w_ff1, w_ff2, ln2_g, ln2_b):
    batch, seq, d_model = x.shape
    assert d_model == D_MODEL and seq % SEQ_TILE == 0 and SEQ_TILE % MLSTM_CHUNK == 0
    n_gate = 2 * N_HEADS
    off_gate_src = 3 * D_MLSTM
    win = jnp.concatenate([
        w_in[:, :, :off_gate_src].astype(BF16),
        w_in[:, :, off_gate_src + n_gate:].astype(BF16),
        w_in[:, :, off_gate_src:off_gate_src + n_gate].astype(BF16),
        jnp.zeros((DEPTH, D_MODEL, LANES - n_gate), BF16)], axis=-1)
    stacked = (
        win,
        jnp.pad(b_gate, ((0, 0), (0, LANES - n_gate)))[:, None, :],
        w_conv, hn_g[:, None, :], w_pool.astype(BF16), pool_scale[:, None, :],
        ln1_g[:, None, :], ln1_b[:, None, :], ln2_g[:, None, :], ln2_b[:, None, :],
    )
    causal = jnp.tril(jnp.ones((MLSTM_CHUNK, MLSTM_CHUNK), dtype=bool))
    consts = (causal.astype(BF16), jnp.where(causal, 0.0, -jnp.inf).astype(F32))
    next_f32 = (w_out, w_ff1, w_ff2)
    big_b = (w_out[0].astype(BF16), w_ff1[0].astype(BF16), w_ff2[0].astype(BF16))
    h = x.reshape(batch * seq, d_model)
    for layer in range(DEPTH):
        h, *big_b = _layer_call(h, stacked, *big_b, consts, next_f32, layer, seq=seq)
    return h.reshape(batch, seq, d_model)
```

```python
import functools

import jax
import jax.numpy as jnp
from jax import lax
from jax.experimental import pallas as pl
from jax.experimental.pallas import tpu as pltpu

D_MODEL = 1024
DEPTH = 4
N_HEADS = 4
HEAD_DIM = 128
D_MLSTM = N_HEADS * HEAD_DIM
POOL_WINDOWS = (2, 4, 8, 16)
POOL_GROUP_DIM = 128
D_POOL = len(POOL_WINDOWS) * POOL_GROUP_DIM
CONV_WIDTH = 4
D_FF = 4 * D_MODEL
ALPHA = (2.0 * DEPTH) ** 0.25
LN_EPS = 1e-5

OFF_Q = 0
OFF_K = OFF_Q + D_MLSTM
OFF_V = OFF_K + D_MLSTM
OFF_O = OFF_V + D_MLSTM
OFF_P = OFF_O + D_MLSTM
OFF_G = OFF_P + D_POOL
LANES = 128
D_IN_PAD = OFF_G + LANES

CONV_HALO = 8
POOL_HALO = 16

SEQ_TILE = 512
MLSTM_CHUNK = 256
FF_CHUNK = 512
VMEM_LIMIT_BYTES = 60000 * 1024

F32 = jnp.float32
BF16 = jnp.bfloat16

PHASE_ORDER = "XMM" + "XM" * 2 + "XMM" * 2 + "XM" * 8 + "XXM" + "XXX" + "M"


def _layer_norm_rows(z, g, b):
    mu = jnp.mean(z, axis=-1, keepdims=True)
    zc = z - mu
    var = jnp.mean(zc * zc, axis=-1, keepdims=True)
    return zc * lax.rsqrt(var + LN_EPS) * g + b


def _cumsum_rows(tri, x):
    p1 = x.astype(BF16)
    r1 = x - p1.astype(F32)
    p2 = r1.astype(BF16)
    p3 = (r1 - p2.astype(F32)).astype(BF16)
    out = jnp.dot(tri, p1, preferred_element_type=F32)
    out = out + jnp.dot(tri, p2, preferred_element_type=F32)
    return out + jnp.dot(tri, p3, preferred_element_type=F32)


def _mlp_phases(z1_buf, g1_ref, b1_ref, w1_ref, w2_ref, g2_ref, b2_ref, o_ref):
    x1 = _layer_norm_rows(z1_buf[...], g1_ref[...], b1_ref[...])
    xb = x1.astype(BF16)
    yield
    acc = None
    for c in range(D_FF // FF_CHUNK):
        cs = slice(c * FF_CHUNK, (c + 1) * FF_CHUNK)
        hid = jnp.dot(xb, w1_ref[:, cs], preferred_element_type=F32)
        yield
        act = jnp.square(jnp.maximum(hid, 0.0)).astype(BF16)
        part = jnp.dot(act, w2_ref[cs, :], preferred_element_type=F32)
        acc = part if acc is None else acc + part
        yield
    o_ref[...] = _layer_norm_rows(ALPHA * x1 + acc, g2_ref[...], b2_ref[...])
    yield


def _mixer_phases(x_ref, t_glob, win_ref, bg_ref, wconv_ref, hng_ref, wpool_ref, pscale_ref,
                  wout_ref, tri_ref, bias_ref, z1_buf,
                  qk_ext, p_ext, vb_ref, o_buf, z_ref, zt_ref, ycat, ct_ref, m_ref,
                  *, ts, chunk):
    n_chunks = ts // chunk
    xb = x_ref[...].astype(BF16)

    qk_ext[CONV_HALO:CONV_HALO + ts, :] = jnp.dot(
        xb, win_ref[:, OFF_Q:OFF_V], preferred_element_type=F32)
    yield
    u_vo = jnp.dot(xb, win_ref[:, OFF_V:OFF_P], preferred_element_type=F32)
    vb_ref[...] = u_vo[:, :D_MLSTM].astype(BF16)
    o_buf[...] = u_vo[:, D_MLSTM:]
    yield
    u_pg = jnp.dot(xb, win_ref[:, OFF_P:D_IN_PAD], preferred_element_type=F32)
    p_ext[POOL_HALO:POOL_HALO + ts, :] = u_pg[:, :D_POOL]
    gates = u_pg[:, D_POOL:] + bg_ref[...]
    logf = jax.nn.log_sigmoid(gates)
    lane = lax.broadcasted_iota(jnp.int32, (chunk, LANES), 1)
    for c in range(n_chunks):
        rs = slice(c * chunk, (c + 1) * chunk)
        a_c = _cumsum_rows(tri_ref[...], logf[rs, :])
        z_c = jnp.where(lane < N_HEADS, gates[rs, :], a_c)
        z_ref[rs, :] = z_c
        zt_ref[:, rs] = z_c.T
    yield

    wc = wconv_ref[...]
    conv_tail = qk_ext[ts:ts + CONV_HALO, :]
    for c in reversed(range(n_chunks)):
        blk = qk_ext[c * chunk:c * chunk + chunk + CONV_HALO, :]
        y = wc[0:1, :] * pltpu.roll(blk, CONV_WIDTH - 1, 0)[CONV_HALO:, :]
        for j in range(1, CONV_WIDTH - 1):
            y = y + wc[j:j + 1, :] * pltpu.roll(blk, CONV_WIDTH - 1 - j, 0)[CONV_HALO:, :]
        y = y + wc[CONV_WIDTH - 1:CONV_WIDTH, :] * blk[CONV_HALO:, :]
        act = jax.nn.silu(y)
        rows = slice(CONV_HALO + c * chunk, CONV_HALO + (c + 1) * chunk)
        qk_ext[rows, OFF_Q:OFF_K] = act[:, OFF_Q:OFF_K]
        qk_ext[rows, OFF_K:OFF_V] = act[:, OFF_K:OFF_V] * (HEAD_DIM ** -0.5)
        yield
    qk_ext[0:CONV_HALO, :] = conv_tail

    ones_col = jnp.where(lane == 0, 1.0, 0.0).astype(BF16)
    for c in range(n_chunks):
        rs = slice(c * chunk, (c + 1) * chunk)
        rows = slice(CONV_HALO + c * chunk, CONV_HALO + (c + 1) * chunk)
        for h in range(N_HEADS):
            hs = slice(h * HEAD_DIM, (h + 1) * HEAD_DIM)
            icol = z_ref[rs, h:h + 1]
            acol = z_ref[rs, N_HEADS + h:N_HEADS + h + 1]
            irow = zt_ref[h:h + 1, rs]
            arow = zt_ref[N_HEADS + h:N_HEADS + h + 1, rs]
            gtot = acol[chunk - 1:chunk, :]
            m_prev = m_ref[h:h + 1, 0:1]
            ct_prev = ct_ref[h]

            q_b = qk_ext[rows, OFF_Q + h * HEAD_DIM:OFF_Q + (h + 1) * HEAD_DIM].astype(BF16)
            k_h = qk_ext[rows, OFF_K + h * HEAD_DIM:OFF_K + (h + 1) * HEAD_DIM]
            k_b = k_h.astype(BF16)
            v_aug = jnp.concatenate([vb_ref[rs, hs], ones_col], axis=1)

            w_row = gtot - arow + irow
            m_loc = jnp.max(w_row, axis=1, keepdims=True)
            e_col = jnp.exp(gtot - acol + icol - m_loc)

            dlog = acol - arow + irow + bias_ref[...]
            m_inter = acol + m_prev
            m_t = jnp.maximum(m_inter, jnp.max(dlog, axis=1, keepdims=True))
            p = jnp.exp(dlog - m_t)
            s_inter = jnp.exp(m_inter - m_t)
            s = lax.dot_general(q_b, k_b, (((1,), (1,)), ((), ())), preferred_element_type=F32)
            sp_b = (s * p).astype(BF16)
            tot = (jnp.dot(sp_b, v_aug, preferred_element_type=F32)
                   + s_inter * jnp.dot(q_b, ct_prev.astype(BF16), preferred_element_type=F32))
            num = tot[:, :HEAD_DIM]
            den = tot[:, HEAD_DIM:HEAD_DIM + 1]
            hh = num / jnp.maximum(jnp.abs(den), jnp.exp(-m_t))

            mu = jnp.mean(hh, axis=1, keepdims=True)
            hc = hh - mu
            var = jnp.mean(hc * hc, axis=1, keepdims=True)
            hn = hc * lax.rsqrt(var + LN_EPS) * hng_ref[:, hs]
            ycat[rs, hs] = (jax.nn.sigmoid(o_buf[rs, hs]) * hn).astype(BF16)

            m_new = jnp.maximum(gtot + m_prev, m_loc)
            s_old = jnp.exp(gtot + m_prev - m_new)
            s_new = jnp.exp(m_loc - m_new)
            ke_t = (e_col * k_h).T.astype(BF16)
            ct_ref[h] = s_old * ct_prev + s_new * jnp.dot(ke_t, v_aug, preferred_element_type=F32)
            m_ref[h:h + 1, :] = jnp.broadcast_to(m_new, (1, LANES))
            yield

    ext = p_ext[...]
    p_ext[0:POOL_HALO, :] = ext[ts:ts + POOL_HALO, :]
    count = (t_glob * ts + 1 + lax.broadcasted_iota(jnp.int32, (ts, 1), 0)).astype(F32)
    wsum = ext
    shift = 1
    for gi, win in enumerate(POOL_WINDOWS):
        while shift < win:
            wsum = wsum + pltpu.roll(wsum, shift, 0)
            shift *= 2
        ls = slice(gi * POOL_GROUP_DIM, (gi + 1) * POOL_GROUP_DIM)
        cur = ext[POOL_HALO:, ls]
        d = wsum[POOL_HALO:, :POOL_GROUP_DIM] / jnp.minimum(count, float(win)) - cur
        yp = jnp.dot(d.astype(BF16), wpool_ref[gi], preferred_element_type=F32) * pscale_ref[:, ls]
        ycat[:, D_MLSTM + gi * POOL_GROUP_DIM:D_MLSTM + (gi + 1) * POOL_GROUP_DIM] = yp.astype(BF16)
        wsum = wsum[:, POOL_GROUP_DIM:]
        yield

    mix = jnp.dot(ycat[...], wout_ref[...], preferred_element_type=F32)
    z1_buf[...] = ALPHA * x_ref[...] + mix
    yield


def _layer_kernel(layer_ref, x_ref, win_ref, bg_ref, wconv_ref, hng_ref, wpool_ref, pscale_ref,
                  g1_ref, b1_ref, g2_ref, b2_ref, wout_ref, w1_ref, w2_ref, tri_ref, bias_ref,
                  nwout_ref, nw1_ref, nw2_ref,
                  o_ref, nwout_b_ref, nw1_b_ref, nw2_b_ref,
                  z1_buf, qk_ext, p_ext, vb_ref, o_buf, z_ref, zt_ref, ycat, ct_ref, m_ref,
                  *, ts, chunk, nt, n_tiles):
    i = pl.program_id(0)
    t_glob = lax.rem(i, nt)

    @pl.when(t_glob == 0)
    def _():
        qk_ext[0:CONV_HALO, :] = jnp.zeros((CONV_HALO, 2 * D_MLSTM), F32)
        p_ext[0:POOL_HALO, :] = jnp.zeros((POOL_HALO, D_POOL), F32)
        ct_ref[...] = jnp.zeros_like(ct_ref)
        m_ref[...] = jnp.zeros_like(m_ref)

    nwout_b_ref[...] = nwout_ref[...].astype(BF16)
    nw1_b_ref[...] = nw1_ref[...].astype(BF16)
    nw2_b_ref[...] = nw2_ref[...].astype(BF16)

    def run(order):
        stages = {
            "X": _mixer_phases(
                x_ref, t_glob, win_ref, bg_ref, wconv_ref, hng_ref, wpool_ref, pscale_ref,
                wout_ref, tri_ref, bias_ref, z1_buf, qk_ext, p_ext, vb_ref, o_buf, z_ref,
                zt_ref, ycat, ct_ref, m_ref, ts=ts, chunk=chunk),
            "M": _mlp_phases(z1_buf, g1_ref, b1_ref, w1_ref, w2_ref, g2_ref, b2_ref, o_ref),
        }
        for tag in order:
            next(stages[tag])
        done = object()
        for tag in set(order):
            assert next(stages[tag], done) is done

    pl.when(jnp.logical_and(i > 0, i < n_tiles))(lambda: run(PHASE_ORDER))
    pl.when(i == 0)(lambda: run(PHASE_ORDER.replace("M", "")))
    pl.when(i == n_tiles)(lambda: run(PHASE_ORDER.replace("X", "")))


def _stacked_spec(shape):
    return pl.BlockSpec((None,) + shape, lambda i, layer: (layer[0],) + (0,) * len(shape),
                        pipeline_mode=pl.Buffered(1))


def _whole_spec(shape):
    return pl.BlockSpec(shape, lambda i, layer: (0,) * len(shape),
                        pipeline_mode=pl.Buffered(1))


def _layer_call(x2d, stacked, wout_b, w1_b, w2_b, consts, next_f32, layer, *, seq):
    ts = SEQ_TILE
    chunk = MLSTM_CHUNK
    n_tiles = x2d.shape[0] // ts
    stacked_shapes = [
        (D_MODEL, D_IN_PAD), (1, LANES), (CONV_WIDTH, 2 * D_MLSTM), (1, D_MLSTM),
        (len(POOL_WINDOWS), POOL_GROUP_DIM, POOL_GROUP_DIM), (1, D_POOL),
        (1, D_MODEL), (1, D_MODEL), (1, D_MODEL), (1, D_MODEL),
    ]
    x_spec = pl.BlockSpec((ts, D_MODEL), lambda i, layer: (jnp.minimum(i, n_tiles - 1), 0))
    o_spec = pl.BlockSpec((ts, D_MODEL), lambda i, layer: (jnp.maximum(i - 1, 0), 0))

    def slab_in(rows, cols):
        return pl.BlockSpec(
            (None, rows // n_tiles, cols),
            lambda i, layer: (jnp.minimum(layer[0] + 1, DEPTH - 1), jnp.minimum(i, n_tiles - 1), 0))

    def slab_out(rows, cols):
        return pl.BlockSpec((rows // n_tiles, cols),
                            lambda i, layer: (jnp.minimum(i, n_tiles - 1), 0))

    big = [(D_MODEL, D_MODEL), (D_MODEL, D_FF), (D_FF, D_MODEL)]
    grid_spec = pltpu.PrefetchScalarGridSpec(
        num_scalar_prefetch=1,
        grid=(n_tiles + 1,),
        in_specs=([x_spec] + [_stacked_spec(s) for s in stacked_shapes]
                  + [_whole_spec(s) for s in big]
                  + [_whole_spec((chunk, chunk)), _whole_spec((chunk, chunk))]
                  + [slab_in(*s) for s in big]),
        out_specs=[o_spec] + [slab_out(*s) for s in big],
        scratch_shapes=[
            pltpu.VMEM((ts, D_MODEL), F32),
            pltpu.VMEM((ts + CONV_HALO, 2 * D_MLSTM), F32),
            pltpu.VMEM((ts + POOL_HALO, D_POOL), F32),
            pltpu.VMEM((ts, D_MLSTM), BF16),
            pltpu.VMEM((ts, D_MLSTM), F32),
            pltpu.VMEM((ts, LANES), F32),
            pltpu.VMEM((LANES, ts), F32),
            pltpu.VMEM((ts, D_MODEL), BF16),
            pltpu.VMEM((N_HEADS, HEAD_DIM, 2 * HEAD_DIM), F32),
            pltpu.VMEM((8, LANES), F32),
        ])
    return pl.pallas_call(
        functools.partial(_layer_kernel, ts=ts, chunk=chunk, nt=seq // ts, n_tiles=n_tiles),
        grid_spec=grid_spec,
        out_shape=[jax.ShapeDtypeStruct(x2d.shape, F32)]
        + [jax.ShapeDtypeStruct(s, BF16) for s in big],
        compiler_params=pltpu.CompilerParams(
            dimension_semantics=("arbitrary",),
            vmem_limit_bytes=VMEM_LIMIT_BYTES),
        name="layer",
    )(jnp.full((1,), layer, jnp.int32), x2d, *stacked, wout_b, w1_b, w2_b, *consts, *next_f32)


def kernel(x, w_in, b_gate, w_conv, hn_g, w_pool, pool_scale, w_out,
           ln1_g, ln1_b, w_ff1, w_ff2, ln2_g, ln2_b):
    batch, seq, d_model = x.shape
    assert d_model == D_MODEL and seq % SEQ_TILE == 0 and SEQ_TILE % MLSTM_CHUNK == 0
    n_gate = 2 * N_HEADS
    off_gate_src = 3 * D_MLSTM
    w_in_b = w_in.astype(BF16)
    win = jnp.concatenate([
        w_in_b[:, :, :off_gate_src],
        w_in_b[:, :, off_gate_src + n_gate:],
        w_in_b[:, :, off_gate_src:off_gate_src + n_gate],
        jnp.zeros((DEPTH, D_MODEL, LANES - n_gate), BF16)], axis=-1)
    stacked = (
        win,
        jnp.pad(b_gate, ((0, 0), (0, LANES - n_gate)))[:, None, :],
        w_conv, hn_g[:, None, :], w_pool.astype(BF16), pool_scale[:, None, :],
        ln1_g[:, None, :], ln1_b[:, None, :], ln2_g[:, None, :], ln2_b[:, None, :],
    )
    causal = jnp.tril(jnp.ones((MLSTM_CHUNK, MLSTM_CHUNK), dtype=bool))
    consts = (causal.astype(BF16), jnp.where(causal, 0.0, -jnp.inf).astype(F32))
    next_f32 = (w_out, w_ff1, w_ff2)
    big_b = (w_out[0].astype(BF16), w_ff1[0].astype(BF16), w_ff2[0].astype(BF16))
    h = x.reshape(batch * seq, d_model)
    for layer in range(DEPTH):
        h, *big_b = _layer_call(h, stacked, *big_b, consts, next_f32, layer, seq=seq)
    return h.reshape(batch, seq, d_model)
```

```python
import functools

import jax
import jax.numpy as jnp
from jax import lax
from jax.experimental import pallas as pl
from jax.experimental.pallas import tpu as pltpu

D_MODEL = 1024
DEPTH = 4
N_HEADS = 4
HEAD_DIM = 128
D_MLSTM = N_HEADS * HEAD_DIM
POOL_WINDOWS = (2, 4, 8, 16)
POOL_GROUP_DIM = 128
D_POOL = len(POOL_WINDOWS) * POOL_GROUP_DIM
CONV_WIDTH = 4
D_FF = 4 * D_MODEL
ALPHA = (2.0 * DEPTH) ** 0.25
LN_EPS = 1e-5

OFF_Q = 0
OFF_K = OFF_Q + D_MLSTM
OFF_V = OFF_K + D_MLSTM
OFF_O = OFF_V + D_MLSTM
OFF_P = OFF_O + D_MLSTM
OFF_G = OFF_P + D_POOL
LANES = 128
D_IN_PAD = OFF_G + LANES

CONV_HALO = 8
POOL_HALO = 16

SEQ_TILE = 512
MLSTM_CHUNK = 256
FF_CHUNK = 1024
VMEM_LIMIT_BYTES = 60000 * 1024

F32 = jnp.float32
BF16 = jnp.bfloat16

PHASE_ORDER = "XMM" + "XM" + "X" + "XM" * 2 + "XXM" * 4 + "XXXXX" + "M"


def _layer_norm_rows(z, g, b):
    mu = jnp.mean(z, axis=-1, keepdims=True)
    zc = z - mu
    var = jnp.mean(zc * zc, axis=-1, keepdims=True)
    return zc * lax.rsqrt(var + LN_EPS) * g + b


def _cumsum_rows(tri, x):
    p1 = x.astype(BF16)
    r1 = x - p1.astype(F32)
    p2 = r1.astype(BF16)
    p3 = (r1 - p2.astype(F32)).astype(BF16)
    out = jnp.dot(tri, p1, preferred_element_type=F32)
    out = out + jnp.dot(tri, p2, preferred_element_type=F32)
    return out + jnp.dot(tri, p3, preferred_element_type=F32)


def _mlp_phases(z1_buf, g1_ref, b1_ref, w1_ref, w2_ref, g2_ref, b2_ref, o_ref):
    x1 = _layer_norm_rows(z1_buf[...], g1_ref[...], b1_ref[...])
    xb = x1.astype(BF16)
    yield
    acc = None
    for c in range(D_FF // FF_CHUNK):
        cs = slice(c * FF_CHUNK, (c + 1) * FF_CHUNK)
        hid = jnp.dot(xb, w1_ref[:, cs], preferred_element_type=F32)
        yield
        act = jnp.maximum(hid.astype(BF16), 0.0)
        act = act * act
        part = jnp.dot(act, w2_ref[cs, :], preferred_element_type=F32)
        acc = part if acc is None else acc + part
        yield
    o_ref[...] = _layer_norm_rows(ALPHA * x1 + acc, g2_ref[...], b2_ref[...])
    yield


def _mixer_phases(x_ref, t_glob, win_ref, bg_ref, wconv_ref, hng_ref, wpool_ref, pscale_ref,
                  wout_ref, tri_ref, bias_ref, z1_buf,
                  qk_ext, p_ext, vb_ref, o_buf, z_ref, zt_ref, ycat, ct_ref, m_ref,
                  *, ts, chunk):
    n_chunks = ts // chunk
    xb = x_ref[...].astype(BF16)

    qk_ext[CONV_HALO:CONV_HALO + ts, :] = jnp.dot(
        xb, win_ref[:, OFF_Q:OFF_V], preferred_element_type=F32)
    yield
    u_vo = jnp.dot(xb, win_ref[:, OFF_V:OFF_P], preferred_element_type=F32)
    vb_ref[...] = u_vo[:, :D_MLSTM].astype(BF16)
    o_buf[...] = u_vo[:, D_MLSTM:]
    yield
    u_pg = jnp.dot(xb, win_ref[:, OFF_P:D_IN_PAD], preferred_element_type=F32)
    p_ext[POOL_HALO:POOL_HALO + ts, :] = u_pg[:, :D_POOL]
    gates = u_pg[:, D_POOL:] + bg_ref[...]
    logf = jax.nn.log_sigmoid(gates)
    lane = lax.broadcasted_iota(jnp.int32, (chunk, LANES), 1)
    for c in range(n_chunks):
        rs = slice(c * chunk, (c + 1) * chunk)
        a_c = _cumsum_rows(tri_ref[...], logf[rs, :])
        z_c = jnp.where(lane < N_HEADS, gates[rs, :], a_c)
        z_ref[rs, :] = z_c
        zt_ref[:, rs] = z_c.T
    yield

    wc = wconv_ref[...]
    conv_tail = qk_ext[ts:ts + CONV_HALO, :]
    for c in reversed(range(n_chunks)):
        blk = qk_ext[c * chunk:c * chunk + chunk + CONV_HALO, :]
        y = wc[0:1, :] * pltpu.roll(blk, CONV_WIDTH - 1, 0)[CONV_HALO:, :]
        for j in range(1, CONV_WIDTH - 1):
            y = y + wc[j:j + 1, :] * pltpu.roll(blk, CONV_WIDTH - 1 - j, 0)[CONV_HALO:, :]
        y = y + wc[CONV_WIDTH - 1:CONV_WIDTH, :] * blk[CONV_HALO:, :]
        act = jax.nn.silu(y)
        rows = slice(CONV_HALO + c * chunk, CONV_HALO + (c + 1) * chunk)
        qk_ext[rows, OFF_Q:OFF_K] = act[:, OFF_Q:OFF_K]
        qk_ext[rows, OFF_K:OFF_V] = act[:, OFF_K:OFF_V] * (HEAD_DIM ** -0.5)
        yield
    qk_ext[0:CONV_HALO, :] = conv_tail

    ones_col = jnp.where(lane == 0, 1.0, 0.0).astype(BF16)
    for c in range(n_chunks):
        rs = slice(c * chunk, (c + 1) * chunk)
        rows = slice(CONV_HALO + c * chunk, CONV_HALO + (c + 1) * chunk)
        for h in range(N_HEADS):
            hs = slice(h * HEAD_DIM, (h + 1) * HEAD_DIM)
            icol = z_ref[rs, h:h + 1]
            acol = z_ref[rs, N_HEADS + h:N_HEADS + h + 1]
            irow = zt_ref[h:h + 1, rs]
            arow = zt_ref[N_HEADS + h:N_HEADS + h + 1, rs]
            gtot = acol[chunk - 1:chunk, :]
            m_prev = m_ref[h:h + 1, 0:1]
            ct_prev = ct_ref[h]

            q_b = qk_ext[rows, OFF_Q + h * HEAD_DIM:OFF_Q + (h + 1) * HEAD_DIM].astype(BF16)
            k_h = qk_ext[rows, OFF_K + h * HEAD_DIM:OFF_K + (h + 1) * HEAD_DIM]
            k_b = k_h.astype(BF16)
            v_aug = jnp.concatenate([vb_ref[rs, hs], ones_col], axis=1)

            w_row = gtot - arow + irow
            m_loc = jnp.max(w_row, axis=1, keepdims=True)
            e_col = jnp.exp(gtot - acol + icol - m_loc)

            dlog = acol - arow + irow + bias_ref[...]
            m_inter = acol + m_prev
            m_t = jnp.maximum(m_inter, jnp.max(dlog, axis=1, keepdims=True))
            p = jnp.exp(dlog - m_t)
            s_inter = jnp.exp(m_inter - m_t)
            s = lax.dot_general(q_b, k_b, (((1,), (1,)), ((), ())), preferred_element_type=F32)
            sp_b = (s * p).astype(BF16)
            tot = (jnp.dot(sp_b, v_aug, preferred_element_type=F32)
                   + s_inter * jnp.dot(q_b, ct_prev.astype(BF16), preferred_element_type=F32))
            num = tot[:, :HEAD_DIM]
            den = tot[:, HEAD_DIM:HEAD_DIM + 1]
            hh = num / jnp.maximum(jnp.abs(den), jnp.exp(-m_t))

            mu = jnp.mean(hh, axis=1, keepdims=True)
            hc = hh - mu
            var = jnp.mean(hc * hc, axis=1, keepdims=True)
            hn = hc * lax.rsqrt(var + LN_EPS) * hng_ref[:, hs]
            ycat[rs, hs] = (jax.nn.sigmoid(o_buf[rs, hs]) * hn).astype(BF16)

            m_new = jnp.maximum(gtot + m_prev, m_loc)
            s_old = jnp.exp(gtot + m_prev - m_new)
            s_new = jnp.exp(m_loc - m_new)
            ke_t = (e_col * k_h).T.astype(BF16)
            ct_ref[h] = s_old * ct_prev + s_new * jnp.dot(ke_t, v_aug, preferred_element_type=F32)
            m_ref[h:h + 1, :] = jnp.broadcast_to(m_new, (1, LANES))
            yield

    ext = p_ext[...]
    p_ext[0:POOL_HALO, :] = ext[ts:ts + POOL_HALO, :]
    count = (t_glob * ts + 1 + lax.broadcasted_iota(jnp.int32, (ts, 1), 0)).astype(F32)
    wsum = ext
    shift = 1
    for gi, win in enumerate(POOL_WINDOWS):
        while shift < win:
            wsum = wsum + pltpu.roll(wsum, shift, 0)
            shift *= 2
        ls = slice(gi * POOL_GROUP_DIM, (gi + 1) * POOL_GROUP_DIM)
        cur = ext[POOL_HALO:, ls]
        d = wsum[POOL_HALO:, :POOL_GROUP_DIM] / jnp.minimum(count, float(win)) - cur
        yp = jnp.dot(d.astype(BF16), wpool_ref[gi], preferred_element_type=F32) * pscale_ref[:, ls]
        ycat[:, D_MLSTM + gi * POOL_GROUP_DIM:D_MLSTM + (gi + 1) * POOL_GROUP_DIM] = yp.astype(BF16)
        wsum = wsum[:, POOL_GROUP_DIM:]
        yield

    mix = jnp.dot(ycat[...], wout_ref[...], preferred_element_type=F32)
    z1_buf[...] = ALPHA * x_ref[...] + mix
    yield


def _layer_kernel(layer_ref, x_ref, win_ref, bg_ref, wconv_ref, hng_ref, wpool_ref, pscale_ref,
                  g1_ref, b1_ref, g2_ref, b2_ref, wout_ref, w1_ref, w2_ref, tri_ref, bias_ref,
                  nwout_ref, nw1_ref, nw2_ref,
                  o_ref, nwout_b_ref, nw1_b_ref, nw2_b_ref,
                  z1_buf, qk_ext, p_ext, vb_ref, o_buf, z_ref, zt_ref, ycat, ct_ref, m_ref,
                  *, ts, chunk, nt, n_tiles):
    i = pl.program_id(0)
    t_glob = lax.rem(i, nt)

    @pl.when(t_glob == 0)
    def _():
        qk_ext[0:CONV_HALO, :] = jnp.zeros((CONV_HALO, 2 * D_MLSTM), F32)
        p_ext[0:POOL_HALO, :] = jnp.zeros((POOL_HALO, D_POOL), F32)
        ct_ref[...] = jnp.zeros_like(ct_ref)
        m_ref[...] = jnp.zeros_like(m_ref)

    nwout_b_ref[...] = nwout_ref[...].astype(BF16)
    nw1_b_ref[...] = nw1_ref[...].astype(BF16)
    nw2_b_ref[...] = nw2_ref[...].astype(BF16)

    def run(order):
        stages = {
            "X": _mixer_phases(
                x_ref, t_glob, win_ref, bg_ref, wconv_ref, hng_ref, wpool_ref, pscale_ref,
                wout_ref, tri_ref, bias_ref, z1_buf, qk_ext, p_ext, vb_ref, o_buf, z_ref,
                zt_ref, ycat, ct_ref, m_ref, ts=ts, chunk=chunk),
            "M": _mlp_phases(z1_buf, g1_ref, b1_ref, w1_ref, w2_ref, g2_ref, b2_ref, o_ref),
        }
        for tag in order:
            next(stages[tag])
        done = object()
        for tag in set(order):
            assert next(stages[tag], done) is done

    pl.when(jnp.logical_and(i > 0, i < n_tiles))(lambda: run(PHASE_ORDER))
    pl.when(i == 0)(lambda: run(PHASE_ORDER.replace("M", "")))
    pl.when(i == n_tiles)(lambda: run(PHASE_ORDER.replace("X", "")))


def _stacked_spec(shape):
    return pl.BlockSpec((None,) + shape, lambda i, layer: (layer[0],) + (0,) * len(shape),
                        pipeline_mode=pl.Buffered(1))


def _whole_spec(shape):
    return pl.BlockSpec(shape, lambda i, layer: (0,) * len(shape),
                        pipeline_mode=pl.Buffered(1))


def _layer_call(x2d, stacked, wout_b, w1_b, w2_b, consts, next_f32, layer, *, seq):
    ts = SEQ_TILE
    chunk = MLSTM_CHUNK
    n_tiles = x2d.shape[0] // ts
    stacked_shapes = [
        (D_MODEL, D_IN_PAD), (1, LANES), (CONV_WIDTH, 2 * D_MLSTM), (1, D_MLSTM),
        (len(POOL_WINDOWS), POOL_GROUP_DIM, POOL_GROUP_DIM), (1, D_POOL),
        (1, D_MODEL), (1, D_MODEL), (1, D_MODEL), (1, D_MODEL),
    ]
    x_spec = pl.BlockSpec((ts, D_MODEL), lambda i, layer: (jnp.minimum(i, n_tiles - 1), 0))
    o_spec = pl.BlockSpec((ts, D_MODEL), lambda i, layer: (jnp.maximum(i - 1, 0), 0))

    def slab_in(rows, cols):
        return pl.BlockSpec(
            (None, rows // n_tiles, cols),
            lambda i, layer: (jnp.minimum(layer[0] + 1, DEPTH - 1), jnp.minimum(i, n_tiles - 1), 0))

    def slab_out(rows, cols):
        return pl.BlockSpec((rows // n_tiles, cols),
                            lambda i, layer: (jnp.minimum(i, n_tiles - 1), 0))

    big = [(D_MODEL, D_MODEL), (D_MODEL, D_FF), (D_FF, D_MODEL)]
    grid_spec = pltpu.PrefetchScalarGridSpec(
        num_scalar_prefetch=1,
        grid=(n_tiles + 1,),
        in_specs=([x_spec] + [_stacked_spec(s) for s in stacked_shapes]
                  + [_whole_spec(s) for s in big]
                  + [_whole_spec((chunk, chunk)), _whole_spec((chunk, chunk))]
                  + [slab_in(*s) for s in big]),
        out_specs=[o_spec] + [slab_out(*s) for s in big],
        scratch_shapes=[
            pltpu.VMEM((ts, D_MODEL), F32),
            pltpu.VMEM((ts + CONV_HALO, 2 * D_MLSTM), F32),
            pltpu.VMEM((ts + POOL_HALO, D_POOL), F32),
            pltpu.VMEM((ts, D_MLSTM), BF16),
            pltpu.VMEM((ts, D_MLSTM), F32),
            pltpu.VMEM((ts, LANES), F32),
            pltpu.VMEM((LANES, ts), F32),
            pltpu.VMEM((ts, D_MODEL), BF16),
            pltpu.VMEM((N_HEADS, HEAD_DIM, 2 * HEAD_DIM), F32),
            pltpu.VMEM((8, LANES), F32),
        ])
    return pl.pallas_call(
        functools.partial(_layer_kernel, ts=ts, chunk=chunk, nt=seq // ts, n_tiles=n_tiles),
        grid_spec=grid_spec,
        out_shape=[jax.ShapeDtypeStruct(x2d.shape, F32)]
        + [jax.ShapeDtypeStruct(s, BF16) for s in big],
        compiler_params=pltpu.CompilerParams(
            dimension_semantics=("arbitrary",),
            vmem_limit_bytes=VMEM_LIMIT_BYTES),
        name="layer",
    )(jnp.full((1,), layer, jnp.int32), x2d, *stacked, wout_b, w1_b, w2_b, *consts, *next_f32)


def kernel(x, w_in, b_gate, w_conv, hn_g, w_pool, pool_scale, w_out,
           ln1_g, ln1_b, w_ff1, w_ff2, ln2_g, ln2_b):
    batch, seq, d_model = x.shape
    assert d_model == D_MODEL and seq % SEQ_TILE == 0 and SEQ_TILE % MLSTM_CHUNK == 0
    n_gate = 2 * N_HEADS
    off_gate_src = 3 * D_MLSTM
    w_in_b = w_in.astype(BF16)
    win = jnp.concatenate([
        w_in_b[:, :, :off_gate_src],
        w_in_b[:, :, off_gate_src + n_gate:],
        w_in_b[:, :, off_gate_src:off_gate_src + n_gate],
        jnp.zeros((DEPTH, D_MODEL, LANES - n_gate), BF16)], axis=-1)
    stacked = (
        win,
        jnp.pad(b_gate, ((0, 0), (0, LANES - n_gate)))[:, None, :],
        w_conv, hn_g[:, None, :], w_pool.astype(BF16), pool_scale[:, None, :],
        ln1_g[:, None, :], ln1_b[:, None, :], ln2_g[:, None, :], ln2_b[:, None, :],
    )
    causal = jnp.tril(jnp.ones((MLSTM_CHUNK, MLSTM_CHUNK), dtype=bool))
    consts = (causal.astype(BF16), jnp.where(causal, 0.0, -jnp.inf).astype(F32))
    next_f32 = (w_out, w_ff1, w_ff2)
    big_b = (w_out[0].astype(BF16), w_ff1[0].astype(BF16), w_ff2[0].astype(BF16))
    h = x.reshape(batch * seq, d_model)
    for layer in range(DEPTH):
        h, *big_b = _layer_call(h, stacked, *big_b, consts, next_f32, layer, seq=seq)
    return h.reshape(batch, seq, d_model)
```

```python
import functools

import jax
import jax.numpy as jnp
from jax import lax
from jax.experimental import pallas as pl
from jax.experimental.pallas import tpu as pltpu

D_MODEL = 1024
DEPTH = 4
N_HEADS = 4
HEAD_DIM = 128
D_MLSTM = N_HEADS * HEAD_DIM
POOL_WINDOWS = (2, 4, 8, 16)
POOL_GROUP_DIM = 128
D_POOL = len(POOL_WINDOWS) * POOL_GROUP_DIM
CONV_WIDTH = 4
D_FF = 4 * D_MODEL
ALPHA = (2.0 * DEPTH) ** 0.25
LN_EPS = 1e-5

OFF_Q = 0
OFF_K = OFF_Q + D_MLSTM
OFF_V = OFF_K + D_MLSTM
D_IN_A = OFF_V + D_MLSTM
OFF_O = 0
OFF_P = OFF_O + D_MLSTM
OFF_G = OFF_P + D_POOL
LANES = 128
D_IN_B = OFF_G + LANES

CONV_HALO = 8
POOL_HALO = 16

SEQ_TILE = 512
MLSTM_CHUNK = 256
FF_CHUNK = 512
VMEM_LIMIT_BYTES = 60000 * 1024

F32 = jnp.float32
BF16 = jnp.bfloat16

PHASE_ORDER = "XMM" + "XM" * 2 + "XMM" * 2 + "XM" * 8 + "XXM" + "XXX" + "M"


def _layer_norm_rows(z, g, b):
    mu = jnp.mean(z, axis=-1, keepdims=True)
    zc = z - mu
    var = jnp.mean(zc * zc, axis=-1, keepdims=True)
    return zc * lax.rsqrt(var + LN_EPS) * g + b


def _cumsum_rows(tri, x):
    p1 = x.astype(BF16)
    r1 = x - p1.astype(F32)
    p2 = r1.astype(BF16)
    p3 = (r1 - p2.astype(F32)).astype(BF16)
    out = jnp.dot(tri, p1, preferred_element_type=F32)
    out = out + jnp.dot(tri, p2, preferred_element_type=F32)
    return out + jnp.dot(tri, p3, preferred_element_type=F32)


def _mlp_phases(z1_buf, g1_ref, b1_ref, w1_ref, w2_ref, g2_ref, b2_ref, o_ref):
    x1 = _layer_norm_rows(z1_buf[...], g1_ref[...], b1_ref[...])
    xb = x1.astype(BF16)
    yield
    acc = None
    for c in range(D_FF // FF_CHUNK):
        cs = slice(c * FF_CHUNK, (c + 1) * FF_CHUNK)
        hid = jnp.dot(xb, w1_ref[:, cs], preferred_element_type=F32)
        yield
        act = jnp.square(jnp.maximum(hid, 0.0)).astype(BF16)
        part = jnp.dot(act, w2_ref[cs, :], preferred_element_type=F32)
        acc = part if acc is None else acc + part
        yield
    o_ref[...] = _layer_norm_rows(ALPHA * x1 + acc, g2_ref[...], b2_ref[...])
    yield


def _mixer_phases(x_ref, t_glob, wina_ref, winb_ref, bg_ref, wconv_ref, hng_ref, wpool_ref, pscale_ref,
                  wout_ref, tri_ref, bias_ref, z1_buf,
                  qk_ext, p_ext, vb_ref, o_buf, z_ref, zt_ref, ycat, ct_ref, m_ref,
                  *, ts, chunk):
    n_chunks = ts // chunk
    xb = x_ref[...].astype(BF16)

    qk_ext[CONV_HALO:CONV_HALO + ts, :] = jnp.dot(
        xb, wina_ref[:, OFF_Q:OFF_V], preferred_element_type=F32)
    yield
    vb_ref[...] = jnp.dot(xb, wina_ref[:, OFF_V:D_IN_A], preferred_element_type=F32).astype(BF16)
    o_buf[...] = jnp.dot(xb, winb_ref[:, OFF_O:OFF_P], preferred_element_type=F32)
    yield
    u_pg = jnp.dot(xb, winb_ref[:, OFF_P:D_IN_B], preferred_element_type=F32)
    p_ext[POOL_HALO:POOL_HALO + ts, :] = u_pg[:, :D_POOL]
    gates = u_pg[:, D_POOL:] + bg_ref[...]
    logf = jax.nn.log_sigmoid(gates)
    lane = lax.broadcasted_iota(jnp.int32, (chunk, LANES), 1)
    for c in range(n_chunks):
        rs = slice(c * chunk, (c + 1) * chunk)
        a_c = _cumsum_rows(tri_ref[...], logf[rs, :])
        z_c = jnp.where(lane < N_HEADS, gates[rs, :], a_c)
        z_ref[rs, :] = z_c
        zt_ref[:, rs] = z_c.T
    yield

    wc = wconv_ref[...]
    conv_tail = qk_ext[ts:ts + CONV_HALO, :]
    for c in reversed(range(n_chunks)):
        blk = qk_ext[c * chunk:c * chunk + chunk + CONV_HALO, :]
        y = wc[0:1, :] * pltpu.roll(blk, CONV_WIDTH - 1, 0)[CONV_HALO:, :]
        for j in range(1, CONV_WIDTH - 1):
            y = y + wc[j:j + 1, :] * pltpu.roll(blk, CONV_WIDTH - 1 - j, 0)[CONV_HALO:, :]
        y = y + wc[CONV_WIDTH - 1:CONV_WIDTH, :] * blk[CONV_HALO:, :]
        act = jax.nn.silu(y)
        rows = slice(CONV_HALO + c * chunk, CONV_HALO + (c + 1) * chunk)
        qk_ext[rows, OFF_Q:OFF_K] = act[:, OFF_Q:OFF_K]
        qk_ext[rows, OFF_K:OFF_V] = act[:, OFF_K:OFF_V] * (HEAD_DIM ** -0.5)
        yield
    qk_ext[0:CONV_HALO, :] = conv_tail

    ones_col = jnp.where(lane == 0, 1.0, 0.0).astype(BF16)
    for c in range(n_chunks):
        rs = slice(c * chunk, (c + 1) * chunk)
        rows = slice(CONV_HALO + c * chunk, CONV_HALO + (c + 1) * chunk)
        for h in range(N_HEADS):
            hs = slice(h * HEAD_DIM, (h + 1) * HEAD_DIM)
            icol = z_ref[rs, h:h + 1]
            acol = z_ref[rs, N_HEADS + h:N_HEADS + h + 1]
            irow = zt_ref[h:h + 1, rs]
            arow = zt_ref[N_HEADS + h:N_HEADS + h + 1, rs]
            gtot = acol[chunk - 1:chunk, :]
            m_prev = m_ref[h:h + 1, 0:1]
            ct_prev = ct_ref[h]

            q_b = qk_ext[rows, OFF_Q + h * HEAD_DIM:OFF_Q + (h + 1) * HEAD_DIM].astype(BF16)
            k_h = qk_ext[rows, OFF_K + h * HEAD_DIM:OFF_K + (h + 1) * HEAD_DIM]
            k_b = k_h.astype(BF16)
            v_aug = jnp.concatenate([vb_ref[rs, hs], ones_col], axis=1)

            w_row = gtot - arow + irow
            m_loc = jnp.max(w_row, axis=1, keepdims=True)
            e_col = jnp.exp(gtot - acol + icol - m_loc)

            dlog = acol - arow + irow + bias_ref[...]
            m_inter = acol + m_prev
            m_t = jnp.maximum(m_inter, jnp.max(dlog, axis=1, keepdims=True))
            p = jnp.exp(dlog - m_t)
            s_inter = jnp.exp(m_inter - m_t)
            s = lax.dot_general(q_b, k_b, (((1,), (1,)), ((), ())), preferred_element_type=F32)
            sp_b = (s * p).astype(BF16)
            tot = (jnp.dot(sp_b, v_aug, preferred_element_type=F32)
                   + s_inter * jnp.dot(q_b, ct_prev.astype(BF16), preferred_element_type=F32))
            num = tot[:, :HEAD_DIM]
            den = tot[:, HEAD_DIM:HEAD_DIM + 1]
            hh = num / jnp.maximum(jnp.abs(den), jnp.exp(-m_t))

            mu = jnp.mean(hh, axis=1, keepdims=True)
            hc = hh - mu
            var = jnp.mean(hc * hc, axis=1, keepdims=True)
            hn = hc * lax.rsqrt(var + LN_EPS) * hng_ref[:, hs]
            ycat[rs, hs] = (jax.nn.sigmoid(o_buf[rs, hs]) * hn).astype(BF16)

            m_new = jnp.maximum(gtot + m_prev, m_loc)
            s_old = jnp.exp(gtot + m_prev - m_new)
            s_new = jnp.exp(m_loc - m_new)
            ke_t = (e_col * k_h).T.astype(BF16)
            ct_ref[h] = s_old * ct_prev + s_new * jnp.dot(ke_t, v_aug, preferred_element_type=F32)
            m_ref[h:h + 1, :] = jnp.broadcast_to(m_new, (1, LANES))
            yield

    ext = p_ext[...]
    p_ext[0:POOL_HALO, :] = ext[ts:ts + POOL_HALO, :]
    count = (t_glob * ts + 1 + lax.broadcasted_iota(jnp.int32, (ts, 1), 0)).astype(F32)
    wsum = ext
    shift = 1
    for gi, win in enumerate(POOL_WINDOWS):
        while shift < win:
            wsum = wsum + pltpu.roll(wsum, shift, 0)
            shift *= 2
        ls = slice(gi * POOL_GROUP_DIM, (gi + 1) * POOL_GROUP_DIM)
        cur = ext[POOL_HALO:, ls]
        d = wsum[POOL_HALO:, :POOL_GROUP_DIM] / jnp.minimum(count, float(win)) - cur
        yp = jnp.dot(d.astype(BF16), wpool_ref[gi], preferred_element_type=F32) * pscale_ref[:, ls]
        ycat[:, D_MLSTM + gi * POOL_GROUP_DIM:D_MLSTM + (gi + 1) * POOL_GROUP_DIM] = yp.astype(BF16)
        wsum = wsum[:, POOL_GROUP_DIM:]
        yield

    mix = jnp.dot(ycat[...], wout_ref[...], preferred_element_type=F32)
    z1_buf[...] = ALPHA * x_ref[...] + mix
    yield


def _layer_kernel(layer_ref, x_ref, wina_ref, winb_ref, bg_ref, wconv_ref, hng_ref, wpool_ref,
                  pscale_ref, g1_ref, b1_ref, g2_ref, b2_ref, wout_ref, w1_ref, w2_ref,
                  tri_ref, bias_ref,
                  nwout_ref, nw1_ref, nw2_ref,
                  o_ref, nwout_b_ref, nw1_b_ref, nw2_b_ref,
                  z1_buf, qk_ext, p_ext, vb_ref, o_buf, z_ref, zt_ref, ycat, ct_ref, m_ref,
                  *, ts, chunk, nt, n_tiles):
    i = pl.program_id(0)
    t_glob = lax.rem(i, nt)

    @pl.when(t_glob == 0)
    def _():
        qk_ext[0:CONV_HALO, :] = jnp.zeros((CONV_HALO, 2 * D_MLSTM), F32)
        p_ext[0:POOL_HALO, :] = jnp.zeros((POOL_HALO, D_POOL), F32)
        ct_ref[...] = jnp.zeros_like(ct_ref)
        m_ref[...] = jnp.zeros_like(m_ref)

    nwout_b_ref[...] = nwout_ref[...].astype(BF16)
    nw1_b_ref[...] = nw1_ref[...].astype(BF16)
    nw2_b_ref[...] = nw2_ref[...].astype(BF16)

    def run(order):
        stages = {
            "X": _mixer_phases(
                x_ref, t_glob, wina_ref, winb_ref, bg_ref, wconv_ref, hng_ref, wpool_ref,
                pscale_ref, wout_ref, tri_ref, bias_ref, z1_buf, qk_ext, p_ext, vb_ref, o_buf,
                z_ref, zt_ref, ycat, ct_ref, m_ref, ts=ts, chunk=chunk),
            "M": _mlp_phases(z1_buf, g1_ref, b1_ref, w1_ref, w2_ref, g2_ref, b2_ref, o_ref),
        }
        for tag in order:
            next(stages[tag])
        done = object()
        for tag in set(order):
            assert next(stages[tag], done) is done

    pl.when(jnp.logical_and(i > 0, i < n_tiles))(lambda: run(PHASE_ORDER))
    pl.when(i == 0)(lambda: run(PHASE_ORDER.replace("M", "")))
    pl.when(i == n_tiles)(lambda: run(PHASE_ORDER.replace("X", "")))


def _stacked_spec(shape):
    return pl.BlockSpec((None,) + shape, lambda i, layer: (layer[0],) + (0,) * len(shape),
                        pipeline_mode=pl.Buffered(1))


def _whole_spec(shape):
    return pl.BlockSpec(shape, lambda i, layer: (0,) * len(shape),
                        pipeline_mode=pl.Buffered(1))


def _layer_call(x2d, stacked, wout_b, w1_b, w2_b, consts, next_f32, layer, *, seq):
    ts = SEQ_TILE
    chunk = MLSTM_CHUNK
    n_tiles = x2d.shape[0] // ts
    stacked_shapes = [
        (D_MODEL, D_IN_A), (D_MODEL, D_IN_B), (1, LANES), (CONV_WIDTH, 2 * D_MLSTM),
        (1, D_MLSTM), (len(POOL_WINDOWS), POOL_GROUP_DIM, POOL_GROUP_DIM), (1, D_POOL),
        (1, D_MODEL), (1, D_MODEL), (1, D_MODEL), (1, D_MODEL),
    ]
    x_spec = pl.BlockSpec((ts, D_MODEL), lambda i, layer: (jnp.minimum(i, n_tiles - 1), 0))
    o_spec = pl.BlockSpec((ts, D_MODEL), lambda i, layer: (jnp.maximum(i - 1, 0), 0))

    def slab_in(rows, cols):
        return pl.BlockSpec(
            (None, rows // n_tiles, cols),
            lambda i, layer: (jnp.minimum(layer[0] + 1, DEPTH - 1), jnp.minimum(i, n_tiles - 1), 0))

    def slab_out(rows, cols):
        return pl.BlockSpec((rows // n_tiles, cols),
                            lambda i, layer: (jnp.minimum(i, n_tiles - 1), 0))

    big = [(D_MODEL, D_MODEL), (D_MODEL, D_FF), (D_FF, D_MODEL)]
    grid_spec = pltpu.PrefetchScalarGridSpec(
        num_scalar_prefetch=1,
        grid=(n_tiles + 1,),
        in_specs=([x_spec] + [_stacked_spec(s) for s in stacked_shapes]
                  + [_whole_spec(s) for s in big]
                  + [_whole_spec((chunk, chunk)), _whole_spec((chunk, chunk))]
                  + [slab_in(*s) for s in big]),
        out_specs=[o_spec] + [slab_out(*s) for s in big],
        scratch_shapes=[
            pltpu.VMEM((ts, D_MODEL), F32),
            pltpu.VMEM((ts + CONV_HALO, 2 * D_MLSTM), F32),
            pltpu.VMEM((ts + POOL_HALO, D_POOL), F32),
            pltpu.VMEM((ts, D_MLSTM), BF16),
            pltpu.VMEM((ts, D_MLSTM), F32),
            pltpu.VMEM((ts, LANES), F32),
            pltpu.VMEM((LANES, ts), F32),
            pltpu.VMEM((ts, D_MODEL), BF16),
            pltpu.VMEM((N_HEADS, HEAD_DIM, 2 * HEAD_DIM), F32),
            pltpu.VMEM((8, LANES), F32),
        ])
    return pl.pallas_call(
        functools.partial(_layer_kernel, ts=ts, chunk=chunk, nt=seq // ts, n_tiles=n_tiles),
        grid_spec=grid_spec,
        out_shape=[jax.ShapeDtypeStruct(x2d.shape, F32)]
        + [jax.ShapeDtypeStruct(s, BF16) for s in big],
        compiler_params=pltpu.CompilerParams(
            dimension_semantics=("arbitrary",),
            vmem_limit_bytes=VMEM_LIMIT_BYTES),
        name="layer",
    )(jnp.full((1,), layer, jnp.int32), x2d, *stacked, wout_b, w1_b, w2_b, *consts, *next_f32)


def kernel(x, w_in, b_gate, w_conv, hn_g, w_pool, pool_scale, w_out,
           ln1_g, ln1_b, w_ff1, w_ff2, ln2_g, ln2_b):
    batch, seq, d_model = x.shape
    assert d_model == D_MODEL and seq % SEQ_TILE == 0 and SEQ_TILE % MLSTM_CHUNK == 0
    n_gate = 2 * N_HEADS
    off_gate_src = 3 * D_MLSTM
    win_a = w_in[:, :, :off_gate_src].astype(BF16)
    win_b = jnp.concatenate([
        w_in[:, :, off_gate_src + n_gate:].astype(BF16),
        w_in[:, :, off_gate_src:off_gate_src + n_gate].astype(BF16),
        jnp.zeros((DEPTH, D_MODEL, LANES - n_gate), BF16)], axis=-1)
    stacked = (
        win_a, win_b,
        jnp.pad(b_gate, ((0, 0), (0, LANES - n_gate)))[:, None, :],
        w_conv, hn_g[:, None, :], w_pool.astype(BF16), pool_scale[:, None, :],
        ln1_g[:, None, :], ln1_b[:, None, :], ln2_g[:, None, :], ln2_b[:, None, :],
    )
    causal = jnp.tril(jnp.ones((MLSTM_CHUNK, MLSTM_CHUNK), dtype=bool))
    consts = (causal.astype(BF16), jnp.where(causal, 0.0, -jnp.inf).astype(F32))
    next_f32 = (w_out, w_ff1, w_ff2)
    big_b = (w_out[0].astype(BF16), w_ff1[0].astype(BF16), w_ff2[0].astype(BF16))
    h = x.reshape(batch * seq, d_model)
    for layer in range(DEPTH):
        h, *big_b = _layer_call(h, stacked, *big_b, consts, next_f32, layer, seq=seq)
    return h.reshape(batch, seq, d_model)
```

```python
import functools

import jax
import jax.numpy as jnp
from jax import lax
from jax.experimental import pallas as pl
from jax.experimental.pallas import tpu as pltpu

D_MODEL = 1024
DEPTH = 4
N_HEADS = 4
HEAD_DIM = 128
D_MLSTM = N_HEADS * HEAD_DIM
POOL_WINDOWS = (2, 4, 8, 16)
POOL_GROUP_DIM = 128
D_POOL = len(POOL_WINDOWS) * POOL_GROUP_DIM
CONV_WIDTH = 4
D_FF = 4 * D_MODEL
ALPHA = (2.0 * DEPTH) ** 0.25
LN_EPS = 1e-5

LANES = 128
N_GATES = 2 * N_HEADS
OFF_Q = 0
OFF_K = OFF_Q + D_MLSTM
OFF_V = OFF_K + D_MLSTM
D_IN_A = OFF_V + D_MLSTM
GATE_LANE = LANES - N_GATES
OFF_O = LANES
OFF_P = OFF_O + D_MLSTM
D_IN_B = OFF_P + D_POOL

CONV_HALO = 8
POOL_HALO = 16

SEQ_TILE = 512
MLSTM_CHUNK = 256
FF_CHUNK = 512
VMEM_LIMIT_BYTES = 60000 * 1024

F32 = jnp.float32
BF16 = jnp.bfloat16

PHASE_ORDER = "XMM" + "XM" * 2 + "XMM" * 2 + "XM" * 8 + "XXM" + "XXX" + "M"


def _layer_norm_rows(z, g, b):
    mu = jnp.mean(z, axis=-1, keepdims=True)
    zc = z - mu
    var = jnp.mean(zc * zc, axis=-1, keepdims=True)
    return zc * lax.rsqrt(var + LN_EPS) * g + b


def _cumsum_rows(tri, x):
    p1 = x.astype(BF16)
    r1 = x - p1.astype(F32)
    p2 = r1.astype(BF16)
    p3 = (r1 - p2.astype(F32)).astype(BF16)
    out = jnp.dot(tri, p1, preferred_element_type=F32)
    out = out + jnp.dot(tri, p2, preferred_element_type=F32)
    return out + jnp.dot(tri, p3, preferred_element_type=F32)


def _mlp_phases(z1_buf, g1_ref, b1_ref, w1_ref, w2_ref, g2_ref, b2_ref, o_ref):
    x1 = _layer_norm_rows(z1_buf[...], g1_ref[...], b1_ref[...])
    xb = x1.astype(BF16)
    yield
    acc = None
    for c in range(D_FF // FF_CHUNK):
        cs = slice(c * FF_CHUNK, (c + 1) * FF_CHUNK)
        hid = jnp.dot(xb, w1_ref[:, cs], preferred_element_type=F32)
        yield
        act = jnp.square(jnp.maximum(hid, 0.0)).astype(BF16)
        part = jnp.dot(act, w2_ref[cs, :], preferred_element_type=F32)
        acc = part if acc is None else acc + part
        yield
    o_ref[...] = _layer_norm_rows(ALPHA * x1 + acc, g2_ref[...], b2_ref[...])
    yield


def _mixer_phases(x_ref, t_glob, wina_ref, winb_ref, bg_ref, wconv_ref, hng_ref, wpool_ref,
                  pscale_ref, wout_ref, tri_ref, bias_ref, z1_buf,
                  qk_ext, p_ext, vb_ref, o_buf, z_ref, zt_ref, ycat, ct_ref, m_ref,
                  *, ts, chunk):
    n_chunks = ts // chunk
    xb = x_ref[...].astype(BF16)

    qk_ext[CONV_HALO:CONV_HALO + ts, :] = jnp.dot(
        xb, wina_ref[:, OFF_Q:OFF_V], preferred_element_type=F32)
    yield
    vb_ref[...] = jnp.dot(xb, wina_ref[:, OFF_V:D_IN_A], preferred_element_type=F32).astype(BF16)
    u_go = jnp.dot(xb, winb_ref[:, :OFF_P], preferred_element_type=F32)
    o_buf[...] = u_go[:, OFF_O:]
    gates = u_go[:, :LANES] + bg_ref[...]
    yield
    p_ext[POOL_HALO:POOL_HALO + ts, :] = jnp.dot(
        xb, winb_ref[:, OFF_P:D_IN_B], preferred_element_type=F32)
    logf = jax.nn.log_sigmoid(gates)
    lane = lax.broadcasted_iota(jnp.int32, (chunk, LANES), 1)
    for c in range(n_chunks):
        rs = slice(c * chunk, (c + 1) * chunk)
        a_c = _cumsum_rows(tri_ref[...], logf[rs, :])
        z_c = jnp.where(lane < GATE_LANE + N_HEADS, gates[rs, :], a_c)
        z_ref[rs, :] = z_c
        zt_ref[:, rs] = z_c.T
    yield

    wc = wconv_ref[...]
    conv_tail = qk_ext[ts:ts + CONV_HALO, :]
    for c in reversed(range(n_chunks)):
        blk = qk_ext[c * chunk:c * chunk + chunk + CONV_HALO, :]
        y = wc[0:1, :] * pltpu.roll(blk, CONV_WIDTH - 1, 0)[CONV_HALO:, :]
        for j in range(1, CONV_WIDTH - 1):
            y = y + wc[j:j + 1, :] * pltpu.roll(blk, CONV_WIDTH - 1 - j, 0)[CONV_HALO:, :]
        y = y + wc[CONV_WIDTH - 1:CONV_WIDTH, :] * blk[CONV_HALO:, :]
        act = jax.nn.silu(y)
        rows = slice(CONV_HALO + c * chunk, CONV_HALO + (c + 1) * chunk)
        qk_ext[rows, OFF_Q:OFF_K] = act[:, OFF_Q:OFF_K]
        qk_ext[rows, OFF_K:OFF_V] = act[:, OFF_K:OFF_V] * (HEAD_DIM ** -0.5)
        yield
    qk_ext[0:CONV_HALO, :] = conv_tail

    ones_col = jnp.where(lane == 0, 1.0, 0.0).astype(BF16)
    for c in range(n_chunks):
        rs = slice(c * chunk, (c + 1) * chunk)
        rows = slice(CONV_HALO + c * chunk, CONV_HALO + (c + 1) * chunk)
        for h in range(N_HEADS):
            hs = slice(h * HEAD_DIM, (h + 1) * HEAD_DIM)
            li, la = GATE_LANE + h, GATE_LANE + N_HEADS + h
            icol = z_ref[rs, li:li + 1]
            acol = z_ref[rs, la:la + 1]
            irow = zt_ref[li:li + 1, rs]
            arow = zt_ref[la:la + 1, rs]
            gtot = acol[chunk - 1:chunk, :]
            m_prev = m_ref[h:h + 1, 0:1]
            ct_prev = ct_ref[h]

            q_b = qk_ext[rows, OFF_Q + h * HEAD_DIM:OFF_Q + (h + 1) * HEAD_DIM].astype(BF16)
            k_h = qk_ext[rows, OFF_K + h * HEAD_DIM:OFF_K + (h + 1) * HEAD_DIM]
            k_b = k_h.astype(BF16)
            v_aug = jnp.concatenate([vb_ref[rs, hs], ones_col], axis=1)

            w_row = gtot - arow + irow
            m_loc = jnp.max(w_row, axis=1, keepdims=True)
            e_col = jnp.exp(gtot - acol + icol - m_loc)

            dlog = acol - arow + irow + bias_ref[...]
            m_inter = acol + m_prev
            m_t = jnp.maximum(m_inter, jnp.max(dlog, axis=1, keepdims=True))
            p = jnp.exp(dlog - m_t)
            s_inter = jnp.exp(m_inter - m_t)
            s = lax.dot_general(q_b, k_b, (((1,), (1,)), ((), ())), preferred_element_type=F32)
            sp_b = (s * p).astype(BF16)
            tot = (jnp.dot(sp_b, v_aug, preferred_element_type=F32)
                   + s_inter * jnp.dot(q_b, ct_prev.astype(BF16), preferred_element_type=F32))
            num = tot[:, :HEAD_DIM]
            den = tot[:, HEAD_DIM:HEAD_DIM + 1]
            hh = num / jnp.maximum(jnp.abs(den), jnp.exp(-m_t))

            mu = jnp.mean(hh, axis=1, keepdims=True)
            hc = hh - mu
            var = jnp.mean(hc * hc, axis=1, keepdims=True)
            hn = hc * lax.rsqrt(var + LN_EPS) * hng_ref[:, hs]
            ycat[rs, hs] = (jax.nn.sigmoid(o_buf[rs, hs]) * hn).astype(BF16)

            m_new = jnp.maximum(gtot + m_prev, m_loc)
            s_old = jnp.exp(gtot + m_prev - m_new)
            s_new = jnp.exp(m_loc - m_new)
            ke_t = (e_col * k_h).T.astype(BF16)
            ct_ref[h] = s_old * ct_prev + s_new * jnp.dot(ke_t, v_aug, preferred_element_type=F32)
            m_ref[h:h + 1, :] = jnp.broadcast_to(m_new, (1, LANES))
            yield

    ext = p_ext[...]
    p_ext[0:POOL_HALO, :] = ext[ts:ts + POOL_HALO, :]
    count = (t_glob * ts + 1 + lax.broadcasted_iota(jnp.int32, (ts, 1), 0)).astype(F32)
    wsum = ext
    shift = 1
    for gi, win in enumerate(POOL_WINDOWS):
        while shift < win:
            wsum = wsum + pltpu.roll(wsum, shift, 0)
            shift *= 2
        ls = slice(gi * POOL_GROUP_DIM, (gi + 1) * POOL_GROUP_DIM)
        cur = ext[POOL_HALO:, ls]
        d = wsum[POOL_HALO:, :POOL_GROUP_DIM] / jnp.minimum(count, float(win)) - cur
        yp = jnp.dot(d.astype(BF16), wpool_ref[gi], preferred_element_type=F32) * pscale_ref[:, ls]
        ycat[:, D_MLSTM + gi * POOL_GROUP_DIM:D_MLSTM + (gi + 1) * POOL_GROUP_DIM] = yp.astype(BF16)
        wsum = wsum[:, POOL_GROUP_DIM:]
        yield

    mix = jnp.dot(ycat[...], wout_ref[...], preferred_element_type=F32)
    z1_buf[...] = ALPHA * x_ref[...] + mix
    yield


def _layer_kernel(layer_ref, x_ref, wina_ref, winb_ref, bg_ref, wconv_ref, hng_ref, wpool_ref,
                  pscale_ref, g1_ref, b1_ref, g2_ref, b2_ref, wout_ref, w1_ref, w2_ref,
                  tri_ref, bias_ref,
                  nwout_ref, nw1_ref, nw2_ref,
                  o_ref, nwout_b_ref, nw1_b_ref, nw2_b_ref,
                  z1_buf, qk_ext, p_ext, vb_ref, o_buf, z_ref, zt_ref, ycat, ct_ref, m_ref,
                  *, ts, chunk, nt, n_tiles):
    i = pl.program_id(0)
    t_glob = lax.rem(i, nt)

    @pl.when(t_glob == 0)
    def _():
        qk_ext[0:CONV_HALO, :] = jnp.zeros((CONV_HALO, 2 * D_MLSTM), F32)
        p_ext[0:POOL_HALO, :] = jnp.zeros((POOL_HALO, D_POOL), F32)
        ct_ref[...] = jnp.zeros_like(ct_ref)
        m_ref[...] = jnp.zeros_like(m_ref)

    nwout_b_ref[...] = nwout_ref[...].astype(BF16)
    nw1_b_ref[...] = nw1_ref[...].astype(BF16)
    nw2_b_ref[...] = nw2_ref[...].astype(BF16)

    def run(order):
        stages = {
            "X": _mixer_phases(
                x_ref, t_glob, wina_ref, winb_ref, bg_ref, wconv_ref, hng_ref, wpool_ref,
                pscale_ref, wout_ref, tri_ref, bias_ref, z1_buf, qk_ext, p_ext, vb_ref, o_buf,
                z_ref, zt_ref, ycat, ct_ref, m_ref, ts=ts, chunk=chunk),
            "M": _mlp_phases(z1_buf, g1_ref, b1_ref, w1_ref, w2_ref, g2_ref, b2_ref, o_ref),
        }
        for tag in order:
            next(stages[tag])
        done = object()
        for tag in set(order):
            assert next(stages[tag], done) is done

    pl.when(jnp.logical_and(i > 0, i < n_tiles))(lambda: run(PHASE_ORDER))
    pl.when(i == 0)(lambda: run(PHASE_ORDER.replace("M", "")))
    pl.when(i == n_tiles)(lambda: run(PHASE_ORDER.replace("X", "")))


def _stacked_spec(shape):
    return pl.BlockSpec((None,) + shape, lambda i, layer: (layer[0],) + (0,) * len(shape),
                        pipeline_mode=pl.Buffered(1))


def _whole_spec(shape):
    return pl.BlockSpec(shape, lambda i, layer: (0,) * len(shape),
                        pipeline_mode=pl.Buffered(1))


def _layer_call(x2d, stacked, wout_b, w1_b, w2_b, consts, next_f32, layer, *, seq):
    ts = SEQ_TILE
    chunk = MLSTM_CHUNK
    n_tiles = x2d.shape[0] // ts
    stacked_shapes = [
        (D_MODEL, D_IN_A), (D_MODEL, D_IN_B), (1, LANES), (CONV_WIDTH, 2 * D_MLSTM),
        (1, D_MLSTM), (len(POOL_WINDOWS), POOL_GROUP_DIM, POOL_GROUP_DIM), (1, D_POOL),
        (1, D_MODEL), (1, D_MODEL), (1, D_MODEL), (1, D_MODEL),
    ]
    x_spec = pl.BlockSpec((ts, D_MODEL), lambda i, layer: (jnp.minimum(i, n_tiles - 1), 0))
    o_spec = pl.BlockSpec((ts, D_MODEL), lambda i, layer: (jnp.maximum(i - 1, 0), 0))

    def slab_in(rows, cols):
        return pl.BlockSpec(
            (None, rows // n_tiles, cols),
            lambda i, layer: (jnp.minimum(layer[0] + 1, DEPTH - 1), jnp.minimum(i, n_tiles - 1), 0))

    def slab_out(rows, cols):
        return pl.BlockSpec((rows // n_tiles, cols),
                            lambda i, layer: (jnp.minimum(i, n_tiles - 1), 0))

    big = [(D_MODEL, D_MODEL), (D_MODEL, D_FF), (D_FF, D_MODEL)]
    grid_spec = pltpu.PrefetchScalarGridSpec(
        num_scalar_prefetch=1,
        grid=(n_tiles + 1,),
        in_specs=([x_spec] + [_stacked_spec(s) for s in stacked_shapes]
                  + [_whole_spec(s) for s in big]
                  + [_whole_spec((chunk, chunk)), _whole_spec((chunk, chunk))]
                  + [slab_in(*s) for s in big]),
        out_specs=[o_spec] + [slab_out(*s) for s in big],
        scratch_shapes=[
            pltpu.VMEM((ts, D_MODEL), F32),
            pltpu.VMEM((ts + CONV_HALO, 2 * D_MLSTM), F32),
            pltpu.VMEM((ts + POOL_HALO, D_POOL), F32),
            pltpu.VMEM((ts, D_MLSTM), BF16),
            pltpu.VMEM((ts, D_MLSTM), F32),
            pltpu.VMEM((ts, LANES), F32),
            pltpu.VMEM((LANES, ts), F32),
            pltpu.VMEM((ts, D_MODEL), BF16),
            pltpu.VMEM((N_HEADS, HEAD_DIM, 2 * HEAD_DIM), F32),
            pltpu.VMEM((8, LANES), F32),
        ])
    return pl.pallas_call(
        functools.partial(_layer_kernel, ts=ts, chunk=chunk, nt=seq // ts, n_tiles=n_tiles),
        grid_spec=grid_spec,
        out_shape=[jax.ShapeDtypeStruct(x2d.shape, F32)]
        + [jax.ShapeDtypeStruct(s, BF16) for s in big],
        compiler_params=pltpu.CompilerParams(
            dimension_semantics=("arbitrary",),
            vmem_limit_bytes=VMEM_LIMIT_BYTES),
        name="layer",
    )(jnp.full((1,), layer, jnp.int32), x2d, *stacked, wout_b, w1_b, w2_b, *consts, *next_f32)


def kernel(x, w_in, b_gate, w_conv, hn_g, w_pool, pool_scale, w_out,
           ln1_g, ln1_b, w_ff1, w_ff2, ln2_g, ln2_b):
    batch, seq, d_model = x.shape
    assert d_model == D_MODEL and seq % SEQ_TILE == 0 and SEQ_TILE % MLSTM_CHUNK == 0
    win_a = w_in[:, :, :D_IN_A].astype(BF16)
    win_b = jnp.pad(w_in[:, :, D_IN_A:].astype(BF16), ((0, 0), (0, 0), (GATE_LANE, 0)))
    stacked = (
        win_a, win_b,
        jnp.pad(b_gate, ((0, 0), (GATE_LANE, 0)))[:, None, :],
        w_conv, hn_g[:, None, :], w_pool.astype(BF16), pool_scale[:, None, :],
        ln1_g[:, None, :], ln1_b[:, None, :], ln2_g[:, None, :], ln2_b[:, None, :],
    )
    causal = jnp.tril(jnp.ones((MLSTM_CHUNK, MLSTM_CHUNK), dtype=bool))
    consts = (causal.astype(BF16), jnp.where(causal, 0.0, -jnp.inf).astype(F32))
    next_f32 = (w_out, w_ff1, w_ff2)
    big_b = (w_out[0].astype(BF16), w_ff1[0].astype(BF16), w_ff2[0].astype(BF16))
    h = x.reshape(batch * seq, d_model)
    for layer in range(DEPTH):
        h, *big_b = _layer_call(h, stacked, *big_b, consts, next_f32, layer, seq=seq)
    return h.reshape(batch, seq, d_model)
```

```python
import functools

import jax
import jax.numpy as jnp
from jax import lax
from jax.experimental import pallas as pl
from jax.experimental.pallas import tpu as pltpu

D_MODEL = 1024
DEPTH = 4
N_HEADS = 4
HEAD_DIM = 128
D_MLSTM = N_HEADS * HEAD_DIM
POOL_WINDOWS = (2, 4, 8, 16)
POOL_GROUP_DIM = 128
D_POOL = len(POOL_WINDOWS) * POOL_GROUP_DIM
CONV_WIDTH = 4
D_FF = 4 * D_MODEL
ALPHA = (2.0 * DEPTH) ** 0.25
LN_EPS = 1e-5

LANES = 128
N_GATES = 2 * N_HEADS
OFF_Q = 0
OFF_K = OFF_Q + D_MLSTM
OFF_V = OFF_K + D_MLSTM
D_IN_A = OFF_V + D_MLSTM
GATE_LANE = LANES - N_GATES
OFF_O = LANES
OFF_P = OFF_O + D_MLSTM
D_IN_B = OFF_P + D_POOL

CONV_HALO = 8
POOL_HALO = 16

SEQ_TILE = 512
MLSTM_CHUNK = 256
FF_CHUNK = 512
VMEM_LIMIT_BYTES = 60000 * 1024

F32 = jnp.float32
BF16 = jnp.bfloat16

PHASE_ORDER = "XMM" + "XM" * 2 + "XMM" * 2 + "XM" * 8 + "XXM" + "XXX" + "M"


def _layer_norm_rows(z, g, b):
    mu = jnp.mean(z, axis=-1, keepdims=True)
    zc = z - mu
    var = jnp.mean(zc * zc, axis=-1, keepdims=True)
    return zc * lax.rsqrt(var + LN_EPS) * g + b


def _cumsum_rows(tri, x):
    p1 = x.astype(BF16)
    r1 = x - p1.astype(F32)
    p2 = r1.astype(BF16)
    p3 = (r1 - p2.astype(F32)).astype(BF16)
    out = jnp.dot(tri, p1, preferred_element_type=F32)
    out = out + jnp.dot(tri, p2, preferred_element_type=F32)
    return out + jnp.dot(tri, p3, preferred_element_type=F32)


def _mlp_phases(z1_buf, g1_ref, b1_ref, w1_ref, w2_ref, g2_ref, b2_ref, o_ref):
    x1 = _layer_norm_rows(z1_buf[...], g1_ref[...], b1_ref[...])
    xb = x1.astype(BF16)
    yield
    acc = None
    for c in range(D_FF // FF_CHUNK):
        cs = slice(c * FF_CHUNK, (c + 1) * FF_CHUNK)
        hid = jnp.dot(xb, w1_ref[:, cs], preferred_element_type=F32)
        yield
        act = jnp.square(jnp.maximum(hid, 0.0)).astype(BF16)
        part = jnp.dot(act, w2_ref[cs, :], preferred_element_type=F32)
        acc = part if acc is None else acc + part
        yield
    o_ref[...] = _layer_norm_rows(ALPHA * x1 + acc, g2_ref[...], b2_ref[...])
    yield


def _mixer_phases(x_ref, t_glob, wina_ref, winb_ref, bg_ref, wconv_ref, hng_ref, wpool_ref,
                  pscale_ref, wout_ref, tri_ref, bias_ref, z1_buf,
                  qk_ext, p_ext, vb_ref, o_buf, z_ref, zt_ref, ycat, ct_ref, m_ref,
                  *, ts, chunk):
    n_chunks = ts // chunk
    xb = x_ref[...].astype(BF16)

    qk_ext[CONV_HALO:CONV_HALO + ts, :] = jnp.dot(
        xb, wina_ref[:, OFF_Q:OFF_V], preferred_element_type=F32)
    yield
    vb_ref[...] = jnp.dot(xb, wina_ref[:, OFF_V:D_IN_A], preferred_element_type=F32).astype(BF16)
    u_go = jnp.dot(xb, winb_ref[:, :OFF_P], preferred_element_type=F32)
    o_buf[...] = u_go[:, OFF_O:]
    gates = u_go[:, :LANES] + bg_ref[...]
    yield
    p_ext[POOL_HALO:POOL_HALO + ts, :] = jnp.dot(
        xb, winb_ref[:, OFF_P:D_IN_B], preferred_element_type=F32)
    logf = jax.nn.log_sigmoid(gates)
    lane = lax.broadcasted_iota(jnp.int32, (chunk, LANES), 1)
    for c in range(n_chunks):
        rs = slice(c * chunk, (c + 1) * chunk)
        a_c = _cumsum_rows(tri_ref[...], logf[rs, :])
        z_c = jnp.where(lane < GATE_LANE + N_HEADS, gates[rs, :], a_c)
        z_ref[rs, :] = z_c
        zt_ref[:, rs] = z_c.T
    yield

    wc = wconv_ref[...]
    conv_tail = qk_ext[ts:ts + CONV_HALO, :]
    for c in reversed(range(n_chunks)):
        blk = qk_ext[c * chunk:c * chunk + chunk + CONV_HALO, :]
        y = wc[0:1, :] * pltpu.roll(blk, CONV_WIDTH - 1, 0)[CONV_HALO:, :]
        for j in range(1, CONV_WIDTH - 1):
            y = y + wc[j:j + 1, :] * pltpu.roll(blk, CONV_WIDTH - 1 - j, 0)[CONV_HALO:, :]
        y = y + wc[CONV_WIDTH - 1:CONV_WIDTH, :] * blk[CONV_HALO:, :]
        act = jax.nn.silu(y)
        rows = slice(CONV_HALO + c * chunk, CONV_HALO + (c + 1) * chunk)
        qk_ext[rows, OFF_Q:OFF_K] = act[:, OFF_Q:OFF_K]
        qk_ext[rows, OFF_K:OFF_V] = act[:, OFF_K:OFF_V] * (HEAD_DIM ** -0.5)
        yield
    qk_ext[0:CONV_HALO, :] = conv_tail

    ones_col = jnp.where(lane == 0, 1.0, 0.0).astype(BF16)
    for c in range(n_chunks):
        rs = slice(c * chunk, (c + 1) * chunk)
        rows = slice(CONV_HALO + c * chunk, CONV_HALO + (c + 1) * chunk)
        for h in range(N_HEADS):
            hs = slice(h * HEAD_DIM, (h + 1) * HEAD_DIM)
            li, la = GATE_LANE + h, GATE_LANE + N_HEADS + h
            icol = z_ref[rs, li:li + 1]
            acol = z_ref[rs, la:la + 1]
            irow = zt_ref[li:li + 1, rs]
            arow = zt_ref[la:la + 1, rs]
            gtot = acol[chunk - 1:chunk, :]
            m_prev = m_ref[h:h + 1, 0:1]
            ct_prev = ct_ref[h]

            q_b = qk_ext[rows, OFF_Q + h * HEAD_DIM:OFF_Q + (h + 1) * HEAD_DIM].astype(BF16)
            k_h = qk_ext[rows, OFF_K + h * HEAD_DIM:OFF_K + (h + 1) * HEAD_DIM]
            k_b = k_h.astype(BF16)
            v_aug = jnp.concatenate([vb_ref[rs, hs], ones_col], axis=1)

            w_row = gtot - arow + irow
            m_loc = jnp.max(w_row, axis=1, keepdims=True)
            e_col = jnp.exp(gtot - acol + icol - m_loc)

            dlog = acol - arow + irow + bias_ref[...]
            m_inter = acol + m_prev
            m_t = jnp.maximum(m_inter, jnp.max(dlog, axis=1, keepdims=True))
            p = jnp.exp(dlog - m_t)
            s_inter = jnp.exp(m_inter - m_t)
            s = lax.dot_general(q_b, k_b, (((1,), (1,)), ((), ())), preferred_element_type=F32)
            sp_b = (s * p).astype(BF16)
            tot = (jnp.dot(sp_b, v_aug, preferred_element_type=F32)
                   + s_inter * jnp.dot(q_b, ct_prev.astype(BF16), preferred_element_type=F32))
            num = tot[:, :HEAD_DIM]
            den = tot[:, HEAD_DIM:HEAD_DIM + 1]
            hh = num / jnp.maximum(jnp.abs(den), jnp.exp(-m_t))

            mu = jnp.mean(hh, axis=1, keepdims=True)
            hc = hh - mu
            var = jnp.mean(hc * hc, axis=1, keepdims=True)
            hn = hc * lax.rsqrt(var + LN_EPS) * hng_ref[:, hs]
            ycat[rs, hs] = (jax.nn.sigmoid(o_buf[rs, hs]) * hn).astype(BF16)

            m_new = jnp.maximum(gtot + m_prev, m_loc)
            s_old = jnp.exp(gtot + m_prev - m_new)
            s_new = jnp.exp(m_loc - m_new)
            ke_t = (e_col * k_h).T.astype(BF16)
            ct_ref[h] = s_old * ct_prev + s_new * jnp.dot(ke_t, v_aug, preferred_element_type=F32)
            m_ref[h:h + 1, :] = jnp.broadcast_to(m_new, (1, LANES))
            yield

    ext = p_ext[...]
    p_ext[0:POOL_HALO, :] = ext[ts:ts + POOL_HALO, :]
    count = (t_glob * ts + 1 + lax.broadcasted_iota(jnp.int32, (ts, 1), 0)).astype(F32)
    wsum = ext
    shift = 1
    for gi, win in enumerate(POOL_WINDOWS):
        while shift < win:
            wsum = wsum + pltpu.roll(wsum, shift, 0)
            shift *= 2
        ls = slice(gi * POOL_GROUP_DIM, (gi + 1) * POOL_GROUP_DIM)
        cur = ext[POOL_HALO:, ls]
        d = wsum[POOL_HALO:, :POOL_GROUP_DIM] / jnp.minimum(count, float(win)) - cur
        yp = jnp.dot(d.astype(BF16), wpool_ref[gi], preferred_element_type=F32) * pscale_ref[:, ls]
        ycat[:, D_MLSTM + gi * POOL_GROUP_DIM:D_MLSTM + (gi + 1) * POOL_GROUP_DIM] = yp.astype(BF16)
        wsum = wsum[:, POOL_GROUP_DIM:]
        yield

    mix = jnp.dot(ycat[...], wout_ref[...], preferred_element_type=F32)
    z1_buf[...] = ALPHA * x_ref[...] + mix
    yield


def _layer_kernel(layer_ref, x_ref, winb_ref, bg_ref, wconv_ref, hng_ref, wpool_ref,
                  pscale_ref, g1_ref, b1_ref, g2_ref, b2_ref,
                  wina_ref, wout_ref, w1_ref, w2_ref, tri_ref, bias_ref,
                  nwina_ref, nwout_ref, nw1_ref, nw2_ref,
                  o_ref, nwina_b_ref, nwout_b_ref, nw1_b_ref, nw2_b_ref,
                  z1_buf, qk_ext, p_ext, vb_ref, o_buf, z_ref, zt_ref, ycat, ct_ref, m_ref,
                  *, ts, chunk, nt, n_tiles):
    i = pl.program_id(0)
    t_glob = lax.rem(i, nt)

    @pl.when(t_glob == 0)
    def _():
        qk_ext[0:CONV_HALO, :] = jnp.zeros((CONV_HALO, 2 * D_MLSTM), F32)
        p_ext[0:POOL_HALO, :] = jnp.zeros((POOL_HALO, D_POOL), F32)
        ct_ref[...] = jnp.zeros_like(ct_ref)
        m_ref[...] = jnp.zeros_like(m_ref)

    nwina_b_ref[...] = nwina_ref[...].astype(BF16)
    nwout_b_ref[...] = nwout_ref[...].astype(BF16)
    nw1_b_ref[...] = nw1_ref[...].astype(BF16)
    nw2_b_ref[...] = nw2_ref[...].astype(BF16)

    def run(order):
        stages = {
            "X": _mixer_phases(
                x_ref, t_glob, wina_ref, winb_ref, bg_ref, wconv_ref, hng_ref, wpool_ref,
                pscale_ref, wout_ref, tri_ref, bias_ref, z1_buf, qk_ext, p_ext, vb_ref, o_buf,
                z_ref, zt_ref, ycat, ct_ref, m_ref, ts=ts, chunk=chunk),
            "M": _mlp_phases(z1_buf, g1_ref, b1_ref, w1_ref, w2_ref, g2_ref, b2_ref, o_ref),
        }
        for tag in order:
            next(stages[tag])
        done = object()
        for tag in set(order):
            assert next(stages[tag], done) is done

    pl.when(jnp.logical_and(i > 0, i < n_tiles))(lambda: run(PHASE_ORDER))
    pl.when(i == 0)(lambda: run(PHASE_ORDER.replace("M", "")))
    pl.when(i == n_tiles)(lambda: run(PHASE_ORDER.replace("X", "")))


def _stacked_spec(shape):
    return pl.BlockSpec((None,) + shape, lambda i, layer: (layer[0],) + (0,) * len(shape),
                        pipeline_mode=pl.Buffered(1))


def _whole_spec(shape):
    return pl.BlockSpec(shape, lambda i, layer: (0,) * len(shape),
                        pipeline_mode=pl.Buffered(1))


def _layer_call(x2d, stacked, big_b, consts, next_f32, layer, *, seq):
    ts = SEQ_TILE
    chunk = MLSTM_CHUNK
    n_tiles = x2d.shape[0] // ts
    stacked_shapes = [
        (D_MODEL, D_IN_B), (1, LANES), (CONV_WIDTH, 2 * D_MLSTM),
        (1, D_MLSTM), (len(POOL_WINDOWS), POOL_GROUP_DIM, POOL_GROUP_DIM), (1, D_POOL),
        (1, D_MODEL), (1, D_MODEL), (1, D_MODEL), (1, D_MODEL),
    ]
    x_spec = pl.BlockSpec((ts, D_MODEL), lambda i, layer: (jnp.minimum(i, n_tiles - 1), 0))
    o_spec = pl.BlockSpec((ts, D_MODEL), lambda i, layer: (jnp.maximum(i - 1, 0), 0))

    def slab_in(rows, cols):
        return pl.BlockSpec(
            (None, rows // n_tiles, cols),
            lambda i, layer: (jnp.minimum(layer[0] + 1, DEPTH - 1), jnp.minimum(i, n_tiles - 1), 0))

    def slab_out(rows, cols):
        return pl.BlockSpec((rows // n_tiles, cols),
                            lambda i, layer: (jnp.minimum(i, n_tiles - 1), 0))

    big = [(D_MODEL, D_IN_A), (D_MODEL, D_MODEL), (D_MODEL, D_FF), (D_FF, D_MODEL)]
    grid_spec = pltpu.PrefetchScalarGridSpec(
        num_scalar_prefetch=1,
        grid=(n_tiles + 1,),
        in_specs=([x_spec] + [_stacked_spec(s) for s in stacked_shapes]
                  + [_whole_spec(s) for s in big]
                  + [_whole_spec((chunk, chunk)), _whole_spec((chunk, chunk))]
                  + [slab_in(*s) for s in big]),
        out_specs=[o_spec] + [slab_out(*s) for s in big],
        scratch_shapes=[
            pltpu.VMEM((ts, D_MODEL), F32),
            pltpu.VMEM((ts + CONV_HALO, 2 * D_MLSTM), F32),
            pltpu.VMEM((ts + POOL_HALO, D_POOL), F32),
            pltpu.VMEM((ts, D_MLSTM), BF16),
            pltpu.VMEM((ts, D_MLSTM), F32),
            pltpu.VMEM((ts, LANES), F32),
            pltpu.VMEM((LANES, ts), F32),
            pltpu.VMEM((ts, D_MODEL), BF16),
            pltpu.VMEM((N_HEADS, HEAD_DIM, 2 * HEAD_DIM), F32),
            pltpu.VMEM((8, LANES), F32),
        ])
    return pl.pallas_call(
        functools.partial(_layer_kernel, ts=ts, chunk=chunk, nt=seq // ts, n_tiles=n_tiles),
        grid_spec=grid_spec,
        out_shape=[jax.ShapeDtypeStruct(x2d.shape, F32)]
        + [jax.ShapeDtypeStruct(s, BF16) for s in big],
        compiler_params=pltpu.CompilerParams(
            dimension_semantics=("arbitrary",),
            vmem_limit_bytes=VMEM_LIMIT_BYTES),
        name="layer",
    )(jnp.full((1,), layer, jnp.int32), x2d, *stacked, *big_b, *consts, *next_f32)


def kernel(x, w_in, b_gate, w_conv, hn_g, w_pool, pool_scale, w_out,
           ln1_g, ln1_b, w_ff1, w_ff2, ln2_g, ln2_b):
    batch, seq, d_model = x.shape
    assert d_model == D_MODEL and seq % SEQ_TILE == 0 and SEQ_TILE % MLSTM_CHUNK == 0
    win_b = jnp.pad(w_in[:, :, D_IN_A:].astype(BF16), ((0, 0), (0, 0), (GATE_LANE, 0)))
    stacked = (
        win_b,
        jnp.pad(b_gate, ((0, 0), (GATE_LANE, 0)))[:, None, :],
        w_conv, hn_g[:, None, :], w_pool.astype(BF16), pool_scale[:, None, :],
        ln1_g[:, None, :], ln1_b[:, None, :], ln2_g[:, None, :], ln2_b[:, None, :],
    )
    causal = jnp.tril(jnp.ones((MLSTM_CHUNK, MLSTM_CHUNK), dtype=bool))
    consts = (causal.astype(BF16), jnp.where(causal, 0.0, -jnp.inf).astype(F32))
    next_f32 = (w_in, w_out, w_ff1, w_ff2)
    big_b = (w_in[0, :, :D_IN_A].astype(BF16), w_out[0].astype(BF16),
             w_ff1[0].astype(BF16), w_ff2[0].astype(BF16))
    h = x.reshape(batch * seq, d_model)
    for layer in range(DEPTH):
        h, *big_b = _layer_call(h, stacked, big_b, consts, next_f32, layer, seq=seq)
    return h.reshape(batch, seq, d_model)
```

```python
import functools

import jax
import jax.numpy as jnp
from jax import lax
from jax.experimental import pallas as pl
from jax.experimental.pallas import tpu as pltpu

D_MODEL = 1024
DEPTH = 4
N_HEADS = 4
HEAD_DIM = 128
D_MLSTM = N_HEADS * HEAD_DIM
POOL_WINDOWS = (2, 4, 8, 16)
POOL_GROUP_DIM = 128
D_POOL = len(POOL_WINDOWS) * POOL_GROUP_DIM
CONV_WIDTH = 4
D_FF = 4 * D_MODEL
ALPHA = (2.0 * DEPTH) ** 0.25
LN_EPS = 1e-5

LANES = 128
N_GATES = 2 * N_HEADS
OFF_Q = 0
OFF_K = OFF_Q + D_MLSTM
OFF_V = OFF_K + D_MLSTM
D_IN_A = OFF_V + D_MLSTM
GATE_LANE = LANES - N_GATES
OFF_O = LANES
OFF_P = OFF_O + D_MLSTM
D_IN_B = OFF_P + D_POOL

CONV_HALO = 8
POOL_HALO = 16

SEQ_TILE = 512
MLSTM_CHUNK = 256
FF_CHUNK = 512
DOWN_GROUP = 256
VMEM_LIMIT_BYTES = 60000 * 1024

F32 = jnp.float32
BF16 = jnp.bfloat16

PHASE_ORDER = "XMM" + "XM" * 2 + "XMM" * 2 + "XM" + "XXM" * 4 + "XXXX" + "M"


def _layer_norm_rows(z, g, b):
    mu = jnp.mean(z, axis=-1, keepdims=True)
    zc = z - mu
    var = jnp.mean(zc * zc, axis=-1, keepdims=True)
    return zc * lax.rsqrt(var + LN_EPS) * g + b


def _cumsum_rows(tri, x):
    p1 = x.astype(BF16)
    r1 = x - p1.astype(F32)
    p2 = r1.astype(BF16)
    p3 = (r1 - p2.astype(F32)).astype(BF16)
    out = jnp.dot(tri, p1, preferred_element_type=F32)
    out = out + jnp.dot(tri, p2, preferred_element_type=F32)
    return out + jnp.dot(tri, p3, preferred_element_type=F32)


def _mlp_phases(z1_buf, g1_ref, b1_ref, w1_ref, w2_ref, g2_ref, b2_ref, act_buf, o_ref):
    x1 = _layer_norm_rows(z1_buf[...], g1_ref[...], b1_ref[...])
    xb = x1.astype(BF16)
    yield

    def act_of(hid):
        return jnp.square(jnp.maximum(hid, 0.0)).astype(BF16)

    n_up = D_FF // FF_CHUNK
    hid_prev = None
    for c in range(n_up):
        hid = jnp.dot(xb, w1_ref[:, c * FF_CHUNK:(c + 1) * FF_CHUNK],
                      preferred_element_type=F32)
        if hid_prev is not None:
            act_buf[:, (c - 1) * FF_CHUNK:c * FF_CHUNK] = act_of(hid_prev)
        hid_prev = hid
        yield
    act_buf[:, (n_up - 1) * FF_CHUNK:] = act_of(hid_prev)

    outs = []
    for j in range(D_MODEL // DOWN_GROUP):
        outs.append(jnp.dot(act_buf[...], w2_ref[:, j * DOWN_GROUP:(j + 1) * DOWN_GROUP],
                            preferred_element_type=F32))
        yield
    y = jnp.concatenate(outs, axis=1)
    o_ref[...] = _layer_norm_rows(ALPHA * x1 + y, g2_ref[...], b2_ref[...])
    yield


def _mixer_phases(x_ref, t_glob, wina_ref, winb_ref, bg_ref, wconv_ref, hng_ref, wpool_ref,
                  pscale_ref, wout_ref, tri_ref, bias_ref, z1_buf,
                  qk_ext, p_ext, vb_ref, o_buf, z_ref, zt_ref, ycat, ct_ref, m_ref,
                  *, ts, chunk):
    n_chunks = ts // chunk
    xb = x_ref[...].astype(BF16)

    qk_ext[CONV_HALO:CONV_HALO + ts, :] = jnp.dot(
        xb, wina_ref[:, OFF_Q:OFF_V], preferred_element_type=F32)
    yield
    vb_ref[...] = jnp.dot(xb, wina_ref[:, OFF_V:D_IN_A], preferred_element_type=F32).astype(BF16)
    u_go = jnp.dot(xb, winb_ref[:, :OFF_P], preferred_element_type=F32)
    o_buf[...] = u_go[:, OFF_O:]
    gates = u_go[:, :LANES] + bg_ref[...]
    yield
    p_ext[POOL_HALO:POOL_HALO + ts, :] = jnp.dot(
        xb, winb_ref[:, OFF_P:D_IN_B], preferred_element_type=F32)
    logf = jax.nn.log_sigmoid(gates)
    lane = lax.broadcasted_iota(jnp.int32, (chunk, LANES), 1)
    for c in range(n_chunks):
        rs = slice(c * chunk, (c + 1) * chunk)
        a_c = _cumsum_rows(tri_ref[...], logf[rs, :])
        z_c = jnp.where(lane < GATE_LANE + N_HEADS, gates[rs, :], a_c)
        z_ref[rs, :] = z_c
        zt_ref[:, rs] = z_c.T
    yield

    wc = wconv_ref[...]
    conv_tail = qk_ext[ts:ts + CONV_HALO, :]
    for c in reversed(range(n_chunks)):
        blk = qk_ext[c * chunk:c * chunk + chunk + CONV_HALO, :]
        y = wc[0:1, :] * pltpu.roll(blk, CONV_WIDTH - 1, 0)[CONV_HALO:, :]
        for j in range(1, CONV_WIDTH - 1):
            y = y + wc[j:j + 1, :] * pltpu.roll(blk, CONV_WIDTH - 1 - j, 0)[CONV_HALO:, :]
        y = y + wc[CONV_WIDTH - 1:CONV_WIDTH, :] * blk[CONV_HALO:, :]
        act = jax.nn.silu(y)
        rows = slice(CONV_HALO + c * chunk, CONV_HALO + (c + 1) * chunk)
        qk_ext[rows, OFF_Q:OFF_K] = act[:, OFF_Q:OFF_K]
        qk_ext[rows, OFF_K:OFF_V] = act[:, OFF_K:OFF_V] * (HEAD_DIM ** -0.5)
        yield
    qk_ext[0:CONV_HALO, :] = conv_tail

    ones_col = jnp.where(lane == 0, 1.0, 0.0).astype(BF16)
    for c in range(n_chunks):
        rs = slice(c * chunk, (c + 1) * chunk)
        rows = slice(CONV_HALO + c * chunk, CONV_HALO + (c + 1) * chunk)
        for h in range(N_HEADS):
            hs = slice(h * HEAD_DIM, (h + 1) * HEAD_DIM)
            li, la = GATE_LANE + h, GATE_LANE + N_HEADS + h
            icol = z_ref[rs, li:li + 1]
            acol = z_ref[rs, la:la + 1]
            irow = zt_ref[li:li + 1, rs]
            arow = zt_ref[la:la + 1, rs]
            gtot = acol[chunk - 1:chunk, :]
            m_prev = m_ref[h:h + 1, 0:1]
            ct_prev = ct_ref[h]

            q_b = qk_ext[rows, OFF_Q + h * HEAD_DIM:OFF_Q + (h + 1) * HEAD_DIM].astype(BF16)
            k_h = qk_ext[rows, OFF_K + h * HEAD_DIM:OFF_K + (h + 1) * HEAD_DIM]
            k_b = k_h.astype(BF16)
            v_aug = jnp.concatenate([vb_ref[rs, hs], ones_col], axis=1)

            w_row = gtot - arow + irow
            m_loc = jnp.max(w_row, axis=1, keepdims=True)
            e_col = jnp.exp(gtot - acol + icol - m_loc)

            dlog = acol - arow + irow + bias_ref[...]
            m_inter = acol + m_prev
            m_t = jnp.maximum(m_inter, jnp.max(dlog, axis=1, keepdims=True))
            p = jnp.exp(dlog - m_t)
            s_inter = jnp.exp(m_inter - m_t)
            s = lax.dot_general(q_b, k_b, (((1,), (1,)), ((), ())), preferred_element_type=F32)
            sp_b = (s * p).astype(BF16)
            tot = (jnp.dot(sp_b, v_aug, preferred_element_type=F32)
                   + s_inter * jnp.dot(q_b, ct_prev.astype(BF16), preferred_element_type=F32))
            num = tot[:, :HEAD_DIM]
            den = tot[:, HEAD_DIM:HEAD_DIM + 1]
            hh = num / jnp.maximum(jnp.abs(den), jnp.exp(-m_t))

            mu = jnp.mean(hh, axis=1, keepdims=True)
            hc = hh - mu
            var = jnp.mean(hc * hc, axis=1, keepdims=True)
            hn = hc * lax.rsqrt(var + LN_EPS) * hng_ref[:, hs]
            ycat[rs, hs] = (jax.nn.sigmoid(o_buf[rs, hs]) * hn).astype(BF16)

            m_new = jnp.maximum(gtot + m_prev, m_loc)
            s_old = jnp.exp(gtot + m_prev - m_new)
            s_new = jnp.exp(m_loc - m_new)
            ke_t = (e_col * k_h).T.astype(BF16)
            ct_ref[h] = s_old * ct_prev + s_new * jnp.dot(ke_t, v_aug, preferred_element_type=F32)
            m_ref[h:h + 1, :] = jnp.broadcast_to(m_new, (1, LANES))
            yield

    ext = p_ext[...]
    p_ext[0:POOL_HALO, :] = ext[ts:ts + POOL_HALO, :]
    count = (t_glob * ts + 1 + lax.broadcasted_iota(jnp.int32, (ts, 1), 0)).astype(F32)
    wsum = ext
    shift = 1
    for gi, win in enumerate(POOL_WINDOWS):
        while shift < win:
            wsum = wsum + pltpu.roll(wsum, shift, 0)
            shift *= 2
        ls = slice(gi * POOL_GROUP_DIM, (gi + 1) * POOL_GROUP_DIM)
        cur = ext[POOL_HALO:, ls]
        d = wsum[POOL_HALO:, :POOL_GROUP_DIM] / jnp.minimum(count, float(win)) - cur
        yp = jnp.dot(d.astype(BF16), wpool_ref[gi], preferred_element_type=F32) * pscale_ref[:, ls]
        ycat[:, D_MLSTM + gi * POOL_GROUP_DIM:D_MLSTM + (gi + 1) * POOL_GROUP_DIM] = yp.astype(BF16)
        wsum = wsum[:, POOL_GROUP_DIM:]
        yield

    mix = jnp.dot(ycat[...], wout_ref[...], preferred_element_type=F32)
    z1_buf[...] = ALPHA * x_ref[...] + mix
    yield


def _layer_kernel(layer_ref, x_ref, wina_ref, winb_ref, bg_ref, wconv_ref, hng_ref, wpool_ref,
                  pscale_ref, g1_ref, b1_ref, g2_ref, b2_ref, wout_ref, w1_ref, w2_ref,
                  tri_ref, bias_ref,
                  nwout_ref, nw1_ref, nw2_ref,
                  o_ref, nwout_b_ref, nw1_b_ref, nw2_b_ref,
                  z1_buf, qk_ext, p_ext, vb_ref, o_buf, z_ref, zt_ref, ycat, ct_ref, m_ref,
                  act_buf, *, ts, chunk, nt, n_tiles):
    i = pl.program_id(0)
    t_glob = lax.rem(i, nt)

    @pl.when(t_glob == 0)
    def _():
        qk_ext[0:CONV_HALO, :] = jnp.zeros((CONV_HALO, 2 * D_MLSTM), F32)
        p_ext[0:POOL_HALO, :] = jnp.zeros((POOL_HALO, D_POOL), F32)
        ct_ref[...] = jnp.zeros_like(ct_ref)
        m_ref[...] = jnp.zeros_like(m_ref)

    nwout_b_ref[...] = nwout_ref[...].astype(BF16)
    nw1_b_ref[...] = nw1_ref[...].astype(BF16)
    nw2_b_ref[...] = nw2_ref[...].astype(BF16)

    def run(order):
        stages = {
            "X": _mixer_phases(
                x_ref, t_glob, wina_ref, winb_ref, bg_ref, wconv_ref, hng_ref, wpool_ref,
                pscale_ref, wout_ref, tri_ref, bias_ref, z1_buf, qk_ext, p_ext, vb_ref, o_buf,
                z_ref, zt_ref, ycat, ct_ref, m_ref, ts=ts, chunk=chunk),
            "M": _mlp_phases(z1_buf, g1_ref, b1_ref, w1_ref, w2_ref, g2_ref, b2_ref, act_buf,
                             o_ref),
        }
        for tag in order:
            next(stages[tag])
        done = object()
        for tag in set(order):
            assert next(stages[tag], done) is done

    pl.when(jnp.logical_and(i > 0, i < n_tiles))(lambda: run(PHASE_ORDER))
    pl.when(i == 0)(lambda: run(PHASE_ORDER.replace("M", "")))
    pl.when(i == n_tiles)(lambda: run(PHASE_ORDER.replace("X", "")))


def _stacked_spec(shape):
    return pl.BlockSpec((None,) + shape, lambda i, layer: (layer[0],) + (0,) * len(shape),
                        pipeline_mode=pl.Buffered(1))


def _whole_spec(shape):
    return pl.BlockSpec(shape, lambda i, layer: (0,) * len(shape),
                        pipeline_mode=pl.Buffered(1))


def _layer_call(x2d, stacked, wout_b, w1_b, w2_b, consts, next_f32, layer, *, seq):
    ts = SEQ_TILE
    chunk = MLSTM_CHUNK
    n_tiles = x2d.shape[0] // ts
    stacked_shapes = [
        (D_MODEL, D_IN_A), (D_MODEL, D_IN_B), (1, LANES), (CONV_WIDTH, 2 * D_MLSTM),
        (1, D_MLSTM), (len(POOL_WINDOWS), POOL_GROUP_DIM, POOL_GROUP_DIM), (1, D_POOL),
        (1, D_MODEL), (1, D_MODEL), (1, D_MODEL), (1, D_MODEL),
    ]
    x_spec = pl.BlockSpec((ts, D_MODEL), lambda i, layer: (jnp.minimum(i, n_tiles - 1), 0))
    o_spec = pl.BlockSpec((ts, D_MODEL), lambda i, layer: (jnp.maximum(i - 1, 0), 0))

    def slab_in(rows, cols):
        return pl.BlockSpec(
            (None, rows // n_tiles, cols),
            lambda i, layer: (jnp.minimum(layer[0] + 1, DEPTH - 1), jnp.minimum(i, n_tiles - 1), 0))

    def slab_out(rows, cols):
        return pl.BlockSpec((rows // n_tiles, cols),
                            lambda i, layer: (jnp.minimum(i, n_tiles - 1), 0))

    big = [(D_MODEL, D_MODEL), (D_MODEL, D_FF), (D_FF, D_MODEL)]
    grid_spec = pltpu.PrefetchScalarGridSpec(
        num_scalar_prefetch=1,
        grid=(n_tiles + 1,),
        in_specs=([x_spec] + [_stacked_spec(s) for s in stacked_shapes]
                  + [_whole_spec(s) for s in big]
                  + [_whole_spec((chunk, chunk)), _whole_spec((chunk, chunk))]
                  + [slab_in(*s) for s in big]),
        out_specs=[o_spec] + [slab_out(*s) for s in big],
        scratch_shapes=[
            pltpu.VMEM((ts, D_MODEL), F32),
            pltpu.VMEM((ts + CONV_HALO, 2 * D_MLSTM), F32),
            pltpu.VMEM((ts + POOL_HALO, D_POOL), F32),
            pltpu.VMEM((ts, D_MLSTM), BF16),
            pltpu.VMEM((ts, D_MLSTM), F32),
            pltpu.VMEM((ts, LANES), F32),
            pltpu.VMEM((LANES, ts), F32),
            pltpu.VMEM((ts, D_MODEL), BF16),
            pltpu.VMEM((N_HEADS, HEAD_DIM, 2 * HEAD_DIM), F32),
            pltpu.VMEM((8, LANES), F32),
            pltpu.VMEM((ts, D_FF), BF16),
        ])
    return pl.pallas_call(
        functools.partial(_layer_kernel, ts=ts, chunk=chunk, nt=seq // ts, n_tiles=n_tiles),
        grid_spec=grid_spec,
        out_shape=[jax.ShapeDtypeStruct(x2d.shape, F32)]
        + [jax.ShapeDtypeStruct(s, BF16) for s in big],
        compiler_params=pltpu.CompilerParams(
            dimension_semantics=("arbitrary",),
            vmem_limit_bytes=VMEM_LIMIT_BYTES),
        name="layer",
    )(jnp.full((1,), layer, jnp.int32), x2d, *stacked, wout_b, w1_b, w2_b, *consts, *next_f32)


def kernel(x, w_in, b_gate, w_conv, hn_g, w_pool, pool_scale, w_out,
           ln1_g, ln1_b, w_ff1, w_ff2, ln2_g, ln2_b):
    batch, seq, d_model = x.shape
    assert d_model == D_MODEL and seq % SEQ_TILE == 0 and SEQ_TILE % MLSTM_CHUNK == 0
    win_a = w_in[:, :, :D_IN_A].astype(BF16)
    win_b = jnp.pad(w_in[:, :, D_IN_A:].astype(BF16), ((0, 0), (0, 0), (GATE_LANE, 0)))
    stacked = (
        win_a, win_b,
        jnp.pad(b_gate, ((0, 0), (GATE_LANE, 0)))[:, None, :],
        w_conv, hn_g[:, None, :], w_pool.astype(BF16), pool_scale[:, None, :],
        ln1_g[:, None, :], ln1_b[:, None, :], ln2_g[:, None, :], ln2_b[:, None, :],
    )
    causal = jnp.tril(jnp.ones((MLSTM_CHUNK, MLSTM_CHUNK), dtype=bool))
    consts = (causal.astype(BF16), jnp.where(causal, 0.0, -jnp.inf).astype(F32))
    next_f32 = (w_out, w_ff1, w_ff2)
    big_b = (w_out[0].astype(BF16), w_ff1[0].astype(BF16), w_ff2[0].astype(BF16))
    h = x.reshape(batch * seq, d_model)
    for layer in range(DEPTH):
        h, *big_b = _layer_call(h, stacked, *big_b, consts, next_f32, layer, seq=seq)
    return h.reshape(batch, seq, d_model)
```

```python
import functools

import jax
import jax.numpy as jnp
from jax import lax
from jax.experimental import pallas as pl
from jax.experimental.pallas import tpu as pltpu

D_MODEL = 1024
DEPTH = 4
N_HEADS = 4
HEAD_DIM = 128
D_MLSTM = N_HEADS * HEAD_DIM
POOL_WINDOWS = (2, 4, 8, 16)
POOL_GROUP_DIM = 128
D_POOL = len(POOL_WINDOWS) * POOL_GROUP_DIM
CONV_WIDTH = 4
D_FF = 4 * D_MODEL
ALPHA = (2.0 * DEPTH) ** 0.25
LN_EPS = 1e-5

LANES = 128
N_GATES = 2 * N_HEADS
OFF_Q = 0
OFF_K = OFF_Q + D_MLSTM
OFF_V = OFF_K + D_MLSTM
D_IN_A = OFF_V + D_MLSTM
GATE_LANE = LANES - N_GATES
OFF_O = LANES
OFF_P = OFF_O + D_MLSTM
D_IN_B = OFF_P + D_POOL

CONV_HALO = 8
POOL_HALO = 16

SEQ_TILE = 512
MLSTM_CHUNK = 256
FF_CHUNK = 1024
DOWN_GROUP = 256
VMEM_LIMIT_BYTES = 60000 * 1024

F32 = jnp.float32
BF16 = jnp.bfloat16

PHASE_ORDER = "XMM" + "XXM" + "XM" * 2 + "XXM" * 4 + "XXXXX" + "M"


def _layer_norm_rows(z, g, b):
    mu = jnp.mean(z, axis=-1, keepdims=True)
    zc = z - mu
    var = jnp.mean(zc * zc, axis=-1, keepdims=True)
    return zc * lax.rsqrt(var + LN_EPS) * g + b


def _cumsum_rows(tri, x):
    p1 = x.astype(BF16)
    r1 = x - p1.astype(F32)
    p2 = r1.astype(BF16)
    p3 = (r1 - p2.astype(F32)).astype(BF16)
    out = jnp.dot(tri, p1, preferred_element_type=F32)
    out = out + jnp.dot(tri, p2, preferred_element_type=F32)
    return out + jnp.dot(tri, p3, preferred_element_type=F32)


def _mlp_phases(z1_buf, g1_ref, b1_ref, w1_ref, w2_ref, g2_ref, b2_ref, act_buf, o_ref):
    x1 = _layer_norm_rows(z1_buf[...], g1_ref[...], b1_ref[...])
    xb = x1.astype(BF16)
    yield

    def act_of(hid):
        return jnp.square(jnp.maximum(hid, 0.0)).astype(BF16)

    n_up = D_FF // FF_CHUNK
    hid_prev = None
    for c in range(n_up):
        hid = jnp.dot(xb, w1_ref[:, c * FF_CHUNK:(c + 1) * FF_CHUNK],
                      preferred_element_type=F32)
        if hid_prev is not None:
            act_buf[:, (c - 1) * FF_CHUNK:c * FF_CHUNK] = act_of(hid_prev)
        hid_prev = hid
        yield
    act_buf[:, (n_up - 1) * FF_CHUNK:] = act_of(hid_prev)

    outs = []
    for j in range(D_MODEL // DOWN_GROUP):
        outs.append(jnp.dot(act_buf[...], w2_ref[:, j * DOWN_GROUP:(j + 1) * DOWN_GROUP],
                            preferred_element_type=F32))
        yield
    y = jnp.concatenate(outs, axis=1)
    o_ref[...] = _layer_norm_rows(ALPHA * x1 + y, g2_ref[...], b2_ref[...])
    yield


def _mixer_phases(x_ref, t_glob, wina_ref, winb_ref, bg_ref, wconv_ref, hng_ref, wpool_ref,
                  pscale_ref, wout_ref, tri_ref, bias_ref, z1_buf,
                  qk_ext, p_ext, vb_ref, o_buf, z_ref, zt_ref, ycat, ct_ref, m_ref,
                  *, ts, chunk):
    n_chunks = ts // chunk
    xb = x_ref[...].astype(BF16)

    qk_ext[CONV_HALO:CONV_HALO + ts, :] = jnp.dot(
        xb, wina_ref[:, OFF_Q:OFF_V], preferred_element_type=F32)
    yield
    vb_ref[...] = jnp.dot(xb, wina_ref[:, OFF_V:D_IN_A], preferred_element_type=F32).astype(BF16)
    u_go = jnp.dot(xb, winb_ref[:, :OFF_P], preferred_element_type=F32)
    o_buf[...] = u_go[:, OFF_O:]
    gates = u_go[:, :LANES] + bg_ref[...]
    yield
    p_ext[POOL_HALO:POOL_HALO + ts, :] = jnp.dot(
        xb, winb_ref[:, OFF_P:D_IN_B], preferred_element_type=F32)
    logf = jax.nn.log_sigmoid(gates)
    lane = lax.broadcasted_iota(jnp.int32, (chunk, LANES), 1)
    for c in range(n_chunks):
        rs = slice(c * chunk, (c + 1) * chunk)
        a_c = _cumsum_rows(tri_ref[...], logf[rs, :])
        z_c = jnp.where(lane < GATE_LANE + N_HEADS, gates[rs, :], a_c)
        z_ref[rs, :] = z_c
        zt_ref[:, rs] = z_c.T
    yield

    wc = wconv_ref[...]
    conv_tail = qk_ext[ts:ts + CONV_HALO, :]
    for c in reversed(range(n_chunks)):
        blk = qk_ext[c * chunk:c * chunk + chunk + CONV_HALO, :]
        y = wc[0:1, :] * pltpu.roll(blk, CONV_WIDTH - 1, 0)[CONV_HALO:, :]
        for j in range(1, CONV_WIDTH - 1):
            y = y + wc[j:j + 1, :] * pltpu.roll(blk, CONV_WIDTH - 1 - j, 0)[CONV_HALO:, :]
        y = y + wc[CONV_WIDTH - 1:CONV_WIDTH, :] * blk[CONV_HALO:, :]
        act = jax.nn.silu(y)
        rows = slice(CONV_HALO + c * chunk, CONV_HALO + (c + 1) * chunk)
        qk_ext[rows, OFF_Q:OFF_K] = act[:, OFF_Q:OFF_K]
        qk_ext[rows, OFF_K:OFF_V] = act[:, OFF_K:OFF_V] * (HEAD_DIM ** -0.5)
        yield
    qk_ext[0:CONV_HALO, :] = conv_tail

    ones_col = jnp.where(lane == 0, 1.0, 0.0).astype(BF16)
    for c in range(n_chunks):
        rs = slice(c * chunk, (c + 1) * chunk)
        rows = slice(CONV_HALO + c * chunk, CONV_HALO + (c + 1) * chunk)
        for h in range(N_HEADS):
            hs = slice(h * HEAD_DIM, (h + 1) * HEAD_DIM)
            li, la = GATE_LANE + h, GATE_LANE + N_HEADS + h
            icol = z_ref[rs, li:li + 1]
            acol = z_ref[rs, la:la + 1]
            irow = zt_ref[li:li + 1, rs]
            arow = zt_ref[la:la + 1, rs]
            gtot = acol[chunk - 1:chunk, :]
            m_prev = m_ref[h:h + 1, 0:1]
            ct_prev = ct_ref[h]

            q_b = qk_ext[rows, OFF_Q + h * HEAD_DIM:OFF_Q + (h + 1) * HEAD_DIM].astype(BF16)
            k_h = qk_ext[rows, OFF_K + h * HEAD_DIM:OFF_K + (h + 1) * HEAD_DIM]
            k_b = k_h.astype(BF16)
            v_aug = jnp.concatenate([vb_ref[rs, hs], ones_col], axis=1)

            w_row = gtot - arow + irow
            m_loc = jnp.max(w_row, axis=1, keepdims=True)
            e_col = jnp.exp(gtot - acol + icol - m_loc)

            dlog = acol - arow + irow + bias_ref[...]
            m_inter = acol + m_prev
            m_t = jnp.maximum(m_inter, jnp.max(dlog, axis=1, keepdims=True))
            p = jnp.exp(dlog - m_t)
            s_inter = jnp.exp(m_inter - m_t)
            s = lax.dot_general(q_b, k_b, (((1,), (1,)), ((), ())), preferred_element_type=F32)
            sp_b = (s * p).astype(BF16)
            tot = (jnp.dot(sp_b, v_aug, preferred_element_type=F32)
                   + s_inter * jnp.dot(q_b, ct_prev.astype(BF16), preferred_element_type=F32))
            num = tot[:, :HEAD_DIM]
            den = tot[:, HEAD_DIM:HEAD_DIM + 1]
            hh = num / jnp.maximum(jnp.abs(den), jnp.exp(-m_t))

            mu = jnp.mean(hh, axis=1, keepdims=True)
            hc = hh - mu
            var = jnp.mean(hc * hc, axis=1, keepdims=True)
            hn = hc * lax.rsqrt(var + LN_EPS) * hng_ref[:, hs]
            ycat[rs, hs] = (jax.nn.sigmoid(o_buf[rs, hs]) * hn).astype(BF16)

            m_new = jnp.maximum(gtot + m_prev, m_loc)
            s_old = jnp.exp(gtot + m_prev - m_new)
            s_new = jnp.exp(m_loc - m_new)
            ke_t = (e_col * k_h).T.astype(BF16)
            ct_ref[h] = s_old * ct_prev + s_new * jnp.dot(ke_t, v_aug, preferred_element_type=F32)
            m_ref[h:h + 1, :] = jnp.broadcast_to(m_new, (1, LANES))
            yield

    ext = p_ext[...]
    p_ext[0:POOL_HALO, :] = ext[ts:ts + POOL_HALO, :]
    count = (t_glob * ts + 1 + lax.broadcasted_iota(jnp.int32, (ts, 1), 0)).astype(F32)
    wsum = ext
    shift = 1
    for gi, win in enumerate(POOL_WINDOWS):
        while shift < win:
            wsum = wsum + pltpu.roll(wsum, shift, 0)
            shift *= 2
        ls = slice(gi * POOL_GROUP_DIM, (gi + 1) * POOL_GROUP_DIM)
        cur = ext[POOL_HALO:, ls]
        d = wsum[POOL_HALO:, :POOL_GROUP_DIM] / jnp.minimum(count, float(win)) - cur
        yp = jnp.dot(d.astype(BF16), wpool_ref[gi], preferred_element_type=F32) * pscale_ref[:, ls]
        ycat[:, D_MLSTM + gi * POOL_GROUP_DIM:D_MLSTM + (gi + 1) * POOL_GROUP_DIM] = yp.astype(BF16)
        wsum = wsum[:, POOL_GROUP_DIM:]
        yield

    mix = jnp.dot(ycat[...], wout_ref[...], preferred_element_type=F32)
    z1_buf[...] = ALPHA * x_ref[...] + mix
    yield


def _layer_kernel(layer_ref, x_ref, wina_ref, winb_ref, bg_ref, wconv_ref, hng_ref, wpool_ref,
                  pscale_ref, g1_ref, b1_ref, g2_ref, b2_ref, wout_ref, w1_ref, w2_ref,
                  tri_ref, bias_ref,
                  nwout_ref, nw1_ref, nw2_ref,
                  o_ref, nwout_b_ref, nw1_b_ref, nw2_b_ref,
                  z1_buf, qk_ext, p_ext, vb_ref, o_buf, z_ref, zt_ref, ycat, ct_ref, m_ref,
                  act_buf, *, ts, chunk, nt, n_tiles):
    i = pl.program_id(0)
    t_glob = lax.rem(i, nt)

    @pl.when(t_glob == 0)
    def _():
        qk_ext[0:CONV_HALO, :] = jnp.zeros((CONV_HALO, 2 * D_MLSTM), F32)
        p_ext[0:POOL_HALO, :] = jnp.zeros((POOL_HALO, D_POOL), F32)
        ct_ref[...] = jnp.zeros_like(ct_ref)
        m_ref[...] = jnp.zeros_like(m_ref)

    nwout_b_ref[...] = nwout_ref[...].astype(BF16)
    nw1_b_ref[...] = nw1_ref[...].astype(BF16)
    nw2_b_ref[...] = nw2_ref[...].astype(BF16)

    def run(order):
        stages = {
            "X": _mixer_phases(
                x_ref, t_glob, wina_ref, winb_ref, bg_ref, wconv_ref, hng_ref, wpool_ref,
                pscale_ref, wout_ref, tri_ref, bias_ref, z1_buf, qk_ext, p_ext, vb_ref, o_buf,
                z_ref, zt_ref, ycat, ct_ref, m_ref, ts=ts, chunk=chunk),
            "M": _mlp_phases(z1_buf, g1_ref, b1_ref, w1_ref, w2_ref, g2_ref, b2_ref, act_buf,
                             o_ref),
        }
        for tag in order:
            next(stages[tag])
        done = object()
        for tag in set(order):
            assert next(stages[tag], done) is done

    pl.when(jnp.logical_and(i > 0, i < n_tiles))(lambda: run(PHASE_ORDER))
    pl.when(i == 0)(lambda: run(PHASE_ORDER.replace("M", "")))
    pl.when(i == n_tiles)(lambda: run(PHASE_ORDER.replace("X", "")))


def _stacked_spec(shape):
    return pl.BlockSpec((None,) + shape, lambda i, layer: (layer[0],) + (0,) * len(shape),
                        pipeline_mode=pl.Buffered(1))


def _whole_spec(shape):
    return pl.BlockSpec(shape, lambda i, layer: (0,) * len(shape),
                        pipeline_mode=pl.Buffered(1))


def _layer_call(x2d, stacked, wout_b, w1_b, w2_b, consts, next_f32, layer, *, seq):
    ts = SEQ_TILE
    chunk = MLSTM_CHUNK
    n_tiles = x2d.shape[0] // ts
    stacked_shapes = [
        (D_MODEL, D_IN_A), (D_MODEL, D_IN_B), (1, LANES), (CONV_WIDTH, 2 * D_MLSTM),
        (1, D_MLSTM), (len(POOL_WINDOWS), POOL_GROUP_DIM, POOL_GROUP_DIM), (1, D_POOL),
        (1, D_MODEL), (1, D_MODEL), (1, D_MODEL), (1, D_MODEL),
    ]
    x_spec = pl.BlockSpec((ts, D_MODEL), lambda i, layer: (jnp.minimum(i, n_tiles - 1), 0))
    o_spec = pl.BlockSpec((ts, D_MODEL), lambda i, layer: (jnp.maximum(i - 1, 0), 0))

    def slab_in(rows, cols):
        return pl.BlockSpec(
            (None, rows // n_tiles, cols),
            lambda i, layer: (jnp.minimum(layer[0] + 1, DEPTH - 1), jnp.minimum(i, n_tiles - 1), 0))

    def slab_out(rows, cols):
        return pl.BlockSpec((rows // n_tiles, cols),
                            lambda i, layer: (jnp.minimum(i, n_tiles - 1), 0))

    big = [(D_MODEL, D_MODEL), (D_MODEL, D_FF), (D_FF, D_MODEL)]
    grid_spec = pltpu.PrefetchScalarGridSpec(
        num_scalar_prefetch=1,
        grid=(n_tiles + 1,),
        in_specs=([x_spec] + [_stacked_spec(s) for s in stacked_shapes]
                  + [_whole_spec(s) for s in big]
                  + [_whole_spec((chunk, chunk)), _whole_spec((chunk, chunk))]
                  + [slab_in(*s) for s in big]),
        out_specs=[o_spec] + [slab_out(*s) for s in big],
        scratch_shapes=[
            pltpu.VMEM((ts, D_MODEL), F32),
            pltpu.VMEM((ts + CONV_HALO, 2 * D_MLSTM), F32),
            pltpu.VMEM((ts + POOL_HALO, D_POOL), F32),
            pltpu.VMEM((ts, D_MLSTM), BF16),
            pltpu.VMEM((ts, D_MLSTM), F32),
            pltpu.VMEM((ts, LANES), F32),
            pltpu.VMEM((LANES, ts), F32),
            pltpu.VMEM((ts, D_MODEL), BF16),
            pltpu.VMEM((N_HEADS, HEAD_DIM, 2 * HEAD_DIM), F32),
            pltpu.VMEM((8, LANES), F32),
            pltpu.VMEM((ts, D_FF), BF16),
        ])
    return pl.pallas_call(
        functools.partial(_layer_kernel, ts=ts, chunk=chunk, nt=seq // ts, n_tiles=n_tiles),
        grid_spec=grid_spec,
        out_shape=[jax.ShapeDtypeStruct(x2d.shape, F32)]
        + [jax.ShapeDtypeStruct(s, BF16) for s in big],
        compiler_params=pltpu.CompilerParams(
            dimension_semantics=("arbitrary",),
            vmem_limit_bytes=VMEM_LIMIT_BYTES),
        name="layer",
    )(jnp.full((1,), layer, jnp.int32), x2d, *stacked, wout_b, w1_b, w2_b, *consts, *next_f32)


def kernel(x, w_in, b_gate, w_conv, hn_g, w_pool, pool_scale, w_out,
           ln1_g, ln1_b, w_ff1, w_ff2, ln2_g, ln2_b):
    batch, seq, d_model = x.shape
    assert d_model == D_MODEL and seq % SEQ_TILE == 0 and SEQ_TILE % MLSTM_CHUNK == 0
    win_a = w_in[:, :, :D_IN_A].astype(BF16)
    win_b = jnp.pad(w_in[:, :, D_IN_A:].astype(BF16), ((0, 0), (0, 0), (GATE_LANE, 0)))
    stacked = (
        win_a, win_b,
        jnp.pad(b_gate, ((0, 0), (GATE_LANE, 0)))[:, None, :],
        w_conv, hn_g[:, None, :], w_pool.astype(BF16), pool_scale[:, None, :],
        ln1_g[:, None, :], ln1_b[:, None, :], ln2_g[:, None, :], ln2_b[:, None, :],
    )
    causal = jnp.tril(jnp.ones((MLSTM_CHUNK, MLSTM_CHUNK), dtype=bool))
    consts = (causal.astype(BF16), jnp.where(causal, 0.0, -jnp.inf).astype(F32))
    next_f32 = (w_out, w_ff1, w_ff2)
    big_b = (w_out[0].astype(BF16), w_ff1[0].astype(BF16), w_ff2[0].astype(BF16))
    h = x.reshape(batch * seq, d_model)
    for layer in range(DEPTH):
        h, *big_b = _layer_call(h, stacked, *big_b, consts, next_f32, layer, seq=seq)
    return h.reshape(batch, seq, d_model)
```

```python
import functools

import jax
import jax.numpy as jnp
from jax import lax
from jax.experimental import pallas as pl
from jax.experimental.pallas import tpu as pltpu

D_MODEL = 1024
DEPTH = 4
N_HEADS = 4
HEAD_DIM = 128
D_MLSTM = N_HEADS * HEAD_DIM
POOL_WINDOWS = (2, 4, 8, 16)
POOL_GROUP_DIM = 128
D_POOL = len(POOL_WINDOWS) * POOL_GROUP_DIM
CONV_WIDTH = 4
D_FF = 4 * D_MODEL
ALPHA = (2.0 * DEPTH) ** 0.25
LN_EPS = 1e-5

LANES = 128
N_GATES = 2 * N_HEADS
OFF_Q = 0
OFF_K = OFF_Q + D_MLSTM
OFF_V = OFF_K + D_MLSTM
D_IN_A = OFF_V + D_MLSTM
GATE_LANE = LANES - N_GATES
OFF_O = LANES
OFF_P = OFF_O + D_MLSTM
D_IN_B = OFF_P + D_POOL

CONV_HALO = 8
POOL_HALO = 16

SEQ_TILE = 512
MLSTM_CHUNK = 256
FF_CHUNK = 1024
DOWN_GROUP = 512
VMEM_LIMIT_BYTES = 60000 * 1024

F32 = jnp.float32
BF16 = jnp.bfloat16

PHASE_ORDER = "XMM" + "XXM" + "XM" * 2 + "XXXXM" * 2 + "XXXXX" + "M"


def _layer_norm_rows(z, g, b):
    mu = jnp.mean(z, axis=-1, keepdims=True)
    zc = z - mu
    var = jnp.mean(zc * zc, axis=-1, keepdims=True)
    return zc * lax.rsqrt(var + LN_EPS) * g + b


def _cumsum_rows(tri, x):
    p1 = x.astype(BF16)
    r1 = x - p1.astype(F32)
    p2 = r1.astype(BF16)
    p3 = (r1 - p2.astype(F32)).astype(BF16)
    out = jnp.dot(tri, p1, preferred_element_type=F32)
    out = out + jnp.dot(tri, p2, preferred_element_type=F32)
    return out + jnp.dot(tri, p3, preferred_element_type=F32)


def _mlp_phases(z1_buf, g1_ref, b1_ref, w1_ref, w2_ref, g2_ref, b2_ref, act_buf, o_ref):
    x1 = _layer_norm_rows(z1_buf[...], g1_ref[...], b1_ref[...])
    xb = x1.astype(BF16)
    yield

    def act_of(hid):
        return jnp.square(jnp.maximum(hid, 0.0)).astype(BF16)

    n_up = D_FF // FF_CHUNK
    hid_prev = None
    for c in range(n_up):
        hid = jnp.dot(xb, w1_ref[:, c * FF_CHUNK:(c + 1) * FF_CHUNK],
                      preferred_element_type=F32)
        if hid_prev is not None:
            act_buf[:, (c - 1) * FF_CHUNK:c * FF_CHUNK] = act_of(hid_prev)
        hid_prev = hid
        yield
    act_buf[:, (n_up - 1) * FF_CHUNK:] = act_of(hid_prev)

    outs = []
    for j in range(D_MODEL // DOWN_GROUP):
        outs.append(jnp.dot(act_buf[...], w2_ref[:, j * DOWN_GROUP:(j + 1) * DOWN_GROUP],
                            preferred_element_type=F32))
        yield
    y = jnp.concatenate(outs, axis=1)
    o_ref[...] = _layer_norm_rows(ALPHA * x1 + y, g2_ref[...], b2_ref[...])
    yield


def _mixer_phases(x_ref, t_glob, wina_ref, winb_ref, bg_ref, wconv_ref, hng_ref, wpool_ref,
                  pscale_ref, wout_ref, tri_ref, bias_ref, z1_buf,
                  qk_ext, p_ext, vb_ref, o_buf, z_ref, zt_ref, ycat, ct_ref, m_ref,
                  *, ts, chunk):
    n_chunks = ts // chunk
    xb = x_ref[...].astype(BF16)

    u_a = jnp.dot(xb, wina_ref[...], preferred_element_type=F32)
    qk_ext[CONV_HALO:CONV_HALO + ts, :] = u_a[:, OFF_Q:OFF_V]
    vb_ref[...] = u_a[:, OFF_V:].astype(BF16)
    yield
    u_b = jnp.dot(xb, winb_ref[...], preferred_element_type=F32)
    gates = u_b[:, :LANES] + bg_ref[...]
    o_buf[...] = u_b[:, OFF_O:OFF_P]
    p_ext[POOL_HALO:POOL_HALO + ts, :] = u_b[:, OFF_P:]
    yield
    logf = jax.nn.log_sigmoid(gates)
    lane = lax.broadcasted_iota(jnp.int32, (chunk, LANES), 1)
    for c in range(n_chunks):
        rs = slice(c * chunk, (c + 1) * chunk)
        a_c = _cumsum_rows(tri_ref[...], logf[rs, :])
        z_c = jnp.where(lane < GATE_LANE + N_HEADS, gates[rs, :], a_c)
        z_ref[rs, :] = z_c
        zt_ref[:, rs] = z_c.T
    yield

    wc = wconv_ref[...]
    conv_tail = qk_ext[ts:ts + CONV_HALO, :]
    for c in reversed(range(n_chunks)):
        blk = qk_ext[c * chunk:c * chunk + chunk + CONV_HALO, :]
        y = wc[0:1, :] * pltpu.roll(blk, CONV_WIDTH - 1, 0)[CONV_HALO:, :]
        for j in range(1, CONV_WIDTH - 1):
            y = y + wc[j:j + 1, :] * pltpu.roll(blk, CONV_WIDTH - 1 - j, 0)[CONV_HALO:, :]
        y = y + wc[CONV_WIDTH - 1:CONV_WIDTH, :] * blk[CONV_HALO:, :]
        act = jax.nn.silu(y)
        rows = slice(CONV_HALO + c * chunk, CONV_HALO + (c + 1) * chunk)
        qk_ext[rows, OFF_Q:OFF_K] = act[:, OFF_Q:OFF_K]
        qk_ext[rows, OFF_K:OFF_V] = act[:, OFF_K:OFF_V] * (HEAD_DIM ** -0.5)
        yield
    qk_ext[0:CONV_HALO, :] = conv_tail

    ones_col = jnp.where(lane == 0, 1.0, 0.0).astype(BF16)
    for c in range(n_chunks):
        rs = slice(c * chunk, (c + 1) * chunk)
        rows = slice(CONV_HALO + c * chunk, CONV_HALO + (c + 1) * chunk)
        for h in range(N_HEADS):
            hs = slice(h * HEAD_DIM, (h + 1) * HEAD_DIM)
            li, la = GATE_LANE + h, GATE_LANE + N_HEADS + h
            icol = z_ref[rs, li:li + 1]
            acol = z_ref[rs, la:la + 1]
            irow = zt_ref[li:li + 1, rs]
            arow = zt_ref[la:la + 1, rs]
            gtot = acol[chunk - 1:chunk, :]
            m_prev = m_ref[h:h + 1, 0:1]
            ct_prev = ct_ref[h]

            q_b = qk_ext[rows, OFF_Q + h * HEAD_DIM:OFF_Q + (h + 1) * HEAD_DIM].astype(BF16)
            k_h = qk_ext[rows, OFF_K + h * HEAD_DIM:OFF_K + (h + 1) * HEAD_DIM]
            k_b = k_h.astype(BF16)
            v_aug = jnp.concatenate([vb_ref[rs, hs], ones_col], axis=1)

            w_row = gtot - arow + irow
            m_loc = jnp.max(w_row, axis=1, keepdims=True)
            e_col = jnp.exp(gtot - acol + icol - m_loc)

            dlog = acol - arow + irow + bias_ref[...]
            m_inter = acol + m_prev
            m_t = jnp.maximum(m_inter, jnp.max(dlog, axis=1, keepdims=True))
            p = jnp.exp(dlog - m_t)
            s_inter = jnp.exp(m_inter - m_t)
            s = lax.dot_general(q_b, k_b, (((1,), (1,)), ((), ())), preferred_element_type=F32)
            sp_b = (s * p).astype(BF16)
            tot = (jnp.dot(sp_b, v_aug, preferred_element_type=F32)
                   + s_inter * jnp.dot(q_b, ct_prev.astype(BF16), preferred_element_type=F32))
            num = tot[:, :HEAD_DIM]
            den = tot[:, HEAD_DIM:HEAD_DIM + 1]
            hh = num / jnp.maximum(jnp.abs(den), jnp.exp(-m_t))

            mu = jnp.mean(hh, axis=1, keepdims=True)
            hc = hh - mu
            var = jnp.mean(hc * hc, axis=1, keepdims=True)
            hn = hc * lax.rsqrt(var + LN_EPS) * hng_ref[:, hs]
            ycat[rs, hs] = (jax.nn.sigmoid(o_buf[rs, hs]) * hn).astype(BF16)

            m_new = jnp.maximum(gtot + m_prev, m_loc)
            s_old = jnp.exp(gtot + m_prev - m_new)
            s_new = jnp.exp(m_loc - m_new)
            ke_t = (e_col * k_h).T.astype(BF16)
            ct_ref[h] = s_old * ct_prev + s_new * jnp.dot(ke_t, v_aug, preferred_element_type=F32)
            m_ref[h:h + 1, :] = jnp.broadcast_to(m_new, (1, LANES))
            yield

    ext = p_ext[...]
    p_ext[0:POOL_HALO, :] = ext[ts:ts + POOL_HALO, :]
    count = (t_glob * ts + 1 + lax.broadcasted_iota(jnp.int32, (ts, 1), 0)).astype(F32)
    wsum = ext
    shift = 1
    for gi, win in enumerate(POOL_WINDOWS):
        while shift < win:
            wsum = wsum + pltpu.roll(wsum, shift, 0)
            shift *= 2
        ls = slice(gi * POOL_GROUP_DIM, (gi + 1) * POOL_GROUP_DIM)
        cur = ext[POOL_HALO:, ls]
        d = wsum[POOL_HALO:, :POOL_GROUP_DIM] / jnp.minimum(count, float(win)) - cur
        yp = jnp.dot(d.astype(BF16), wpool_ref[gi], preferred_element_type=F32) * pscale_ref[:, ls]
        ycat[:, D_MLSTM + gi * POOL_GROUP_DIM:D_MLSTM + (gi + 1) * POOL_GROUP_DIM] = yp.astype(BF16)
        wsum = wsum[:, POOL_GROUP_DIM:]
        yield

    mix = jnp.dot(ycat[...], wout_ref[...], preferred_element_type=F32)
    z1_buf[...] = ALPHA * x_ref[...] + mix
    yield


def _layer_kernel(layer_ref, x_ref, wina_ref, winb_ref, bg_ref, wconv_ref, hng_ref, wpool_ref,
                  pscale_ref, g1_ref, b1_ref, g2_ref, b2_ref, wout_ref, w1_ref, w2_ref,
                  tri_ref, bias_ref,
                  nwout_ref, nw1_ref, nw2_ref,
                  o_ref, nwout_b_ref, nw1_b_ref, nw2_b_ref,
                  z1_buf, qk_ext, p_ext, vb_ref, o_buf, z_ref, zt_ref, ycat, ct_ref, m_ref,
                  act_buf, *, ts, chunk, nt, n_tiles):
    i = pl.program_id(0)
    t_glob = lax.rem(i, nt)

    @pl.when(t_glob == 0)
    def _():
        qk_ext[0:CONV_HALO, :] = jnp.zeros((CONV_HALO, 2 * D_MLSTM), F32)
        p_ext[0:POOL_HALO, :] = jnp.zeros((POOL_HALO, D_POOL), F32)
        ct_ref[...] = jnp.zeros_like(ct_ref)
        m_ref[...] = jnp.zeros_like(m_ref)

    nwout_b_ref[...] = nwout_ref[...].astype(BF16)
    nw1_b_ref[...] = nw1_ref[...].astype(BF16)
    nw2_b_ref[...] = nw2_ref[...].astype(BF16)

    def run(order):
        stages = {
            "X": _mixer_phases(
                x_ref, t_glob, wina_ref, winb_ref, bg_ref, wconv_ref, hng_ref, wpool_ref,
                pscale_ref, wout_ref, tri_ref, bias_ref, z1_buf, qk_ext, p_ext, vb_ref, o_buf,
                z_ref, zt_ref, ycat, ct_ref, m_ref, ts=ts, chunk=chunk),
            "M": _mlp_phases(z1_buf, g1_ref, b1_ref, w1_ref, w2_ref, g2_ref, b2_ref, act_buf,
                             o_ref),
        }
        for tag in order:
            next(stages[tag])
        done = object()
        for tag in set(order):
            assert next(stages[tag], done) is done

    pl.when(jnp.logical_and(i > 0, i < n_tiles))(lambda: run(PHASE_ORDER))
    pl.when(i == 0)(lambda: run(PHASE_ORDER.replace("M", "")))
    pl.when(i == n_tiles)(lambda: run(PHASE_ORDER.replace("X", "")))


def _stacked_spec(shape):
    return pl.BlockSpec((None,) + shape, lambda i, layer: (layer[0],) + (0,) * len(shape),
                        pipeline_mode=pl.Buffered(1))


def _whole_spec(shape):
    return pl.BlockSpec(shape, lambda i, layer: (0,) * len(shape),
                        pipeline_mode=pl.Buffered(1))


def _layer_call(x2d, stacked, wout_b, w1_b, w2_b, consts, next_f32, layer, *, seq):
    ts = SEQ_TILE
    chunk = MLSTM_CHUNK
    n_tiles = x2d.shape[0] // ts
    stacked_shapes = [
        (D_MODEL, D_IN_A), (D_MODEL, D_IN_B), (1, LANES), (CONV_WIDTH, 2 * D_MLSTM),
        (1, D_MLSTM), (len(POOL_WINDOWS), POOL_GROUP_DIM, POOL_GROUP_DIM), (1, D_POOL),
        (1, D_MODEL), (1, D_MODEL), (1, D_MODEL), (1, D_MODEL),
    ]
    x_spec = pl.BlockSpec((ts, D_MODEL), lambda i, layer: (jnp.minimum(i, n_tiles - 1), 0))
    o_spec = pl.BlockSpec((ts, D_MODEL), lambda i, layer: (jnp.maximum(i - 1, 0), 0))

    def slab_in(rows, cols):
        return pl.BlockSpec(
            (None, rows // n_tiles, cols),
            lambda i, layer: (jnp.minimum(layer[0] + 1, DEPTH - 1), jnp.minimum(i, n_tiles - 1), 0))

    def slab_out(rows, cols):
        return pl.BlockSpec((rows // n_tiles, cols),
                            lambda i, layer: (jnp.minimum(i, n_tiles - 1), 0))

    big = [(D_MODEL, D_MODEL), (D_MODEL, D_FF), (D_FF, D_MODEL)]
    grid_spec = pltpu.PrefetchScalarGridSpec(
        num_scalar_prefetch=1,
        grid=(n_tiles + 1,),
        in_specs=([x_spec] + [_stacked_spec(s) for s in stacked_shapes]
                  + [_whole_spec(s) for s in big]
                  + [_whole_spec((chunk, chunk)), _whole_spec((chunk, chunk))]
                  + [slab_in(*s) for s in big]),
        out_specs=[o_spec] + [slab_out(*s) for s in big],
        scratch_shapes=[
            pltpu.VMEM((ts, D_MODEL), F32),
            pltpu.VMEM((ts + CONV_HALO, 2 * D_MLSTM), F32),
            pltpu.VMEM((ts + POOL_HALO, D_POOL), F32),
            pltpu.VMEM((ts, D_MLSTM), BF16),
            pltpu.VMEM((ts, D_MLSTM), F32),
            pltpu.VMEM((ts, LANES), F32),
            pltpu.VMEM((LANES, ts), F32),
            pltpu.VMEM((ts, D_MODEL), BF16),
            pltpu.VMEM((N_HEADS, HEAD_DIM, 2 * HEAD_DIM), F32),
            pltpu.VMEM((8, LANES), F32),
            pltpu.VMEM((ts, D_FF), BF16),
        ])
    return pl.pallas_call(
        functools.partial(_layer_kernel, ts=ts, chunk=chunk, nt=seq // ts, n_tiles=n_tiles),
        grid_spec=grid_spec,
        out_shape=[jax.ShapeDtypeStruct(x2d.shape, F32)]
        + [jax.ShapeDtypeStruct(s, BF16) for s in big],
        compiler_params=pltpu.CompilerParams(
            dimension_semantics=("arbitrary",),
            vmem_limit_bytes=VMEM_LIMIT_BYTES),
        name="layer",
    )(jnp.full((1,), layer, jnp.int32), x2d, *stacked, wout_b, w1_b, w2_b, *consts, *next_f32)


def kernel(x, w_in, b_gate, w_conv, hn_g, w_pool, pool_scale, w_out,
           ln1_g, ln1_b, w_ff1, w_ff2, ln2_g, ln2_b):
    batch, seq, d_model = x.shape
    assert d_model == D_MODEL and seq % SEQ_TILE == 0 and SEQ_TILE % MLSTM_CHUNK == 0
    win_a = w_in[:, :, :D_IN_A].astype(BF16)
    win_b = jnp.pad(w_in[:, :, D_IN_A:].astype(BF16), ((0, 0), (0, 0), (GATE_LANE, 0)))
    stacked = (
        win_a, win_b,
        jnp.pad(b_gate, ((0, 0), (GATE_LANE, 0)))[:, None, :],
        w_conv, hn_g[:, None, :], w_pool.astype(BF16), pool_scale[:, None, :],
        ln1_g[:, None, :], ln1_b[:, None, :], ln2_g[:, None, :], ln2_b[:, None, :],
    )
    causal = jnp.tril(jnp.ones((MLSTM_CHUNK, MLSTM_CHUNK), dtype=bool))
    consts = (causal.astype(BF16), jnp.where(causal, 0.0, -jnp.inf).astype(F32))
    next_f32 = (w_out, w_ff1, w_ff2)
    big_b = (w_out[0].astype(BF16), w_ff1[0].astype(BF16), w_ff2[0].astype(BF16))
    h = x.reshape(batch * seq, d_model)
    for layer in range(DEPTH):
        h, *big_b = _layer_call(h, stacked, *big_b, consts, next_f32, layer, seq=seq)
    return h.reshape(batch, seq, d_model)
```

```python
import functools

import jax
import jax.numpy as jnp
from jax import lax
from jax.experimental import pallas as pl
from jax.experimental.pallas import tpu as pltpu

D_MODEL = 1024
DEPTH = 4
N_HEADS = 4
HEAD_DIM = 128
D_MLSTM = N_HEADS * HEAD_DIM
POOL_WINDOWS = (2, 4, 8, 16)
POOL_GROUP_DIM = 128
D_POOL = len(POOL_WINDOWS) * POOL_GROUP_DIM
CONV_WIDTH = 4
D_FF = 4 * D_MODEL
ALPHA = (2.0 * DEPTH) ** 0.25
LN_EPS = 1e-5

LANES = 128
N_GATES = 2 * N_HEADS
OFF_Q = 0
OFF_K = OFF_Q + D_MLSTM
OFF_V = OFF_K + D_MLSTM
D_IN_A = OFF_V + D_MLSTM
GATE_LANE = LANES - N_GATES
OFF_O = LANES
OFF_P = OFF_O + D_MLSTM
D_IN_B = OFF_P + D_POOL

CONV_HALO = 8
POOL_HALO = 16

SEQ_TILE = 512
MLSTM_CHUNK = 256
FF_CHUNK = 1024
DOWN_GROUP = 256
VMEM_LIMIT_BYTES = 60000 * 1024

F32 = jnp.float32
BF16 = jnp.bfloat16

PHASE_ORDER = "XMM" + "XXM" + "XM" * 2 + "XM" * 4 + "XXXXX" + "M"


def _layer_norm_rows(z, g, b):
    mu = jnp.mean(z, axis=-1, keepdims=True)
    zc = z - mu
    var = jnp.mean(zc * zc, axis=-1, keepdims=True)
    return zc * lax.rsqrt(var + LN_EPS) * g + b


def _cumsum_rows(tri, x):
    p1 = x.astype(BF16)
    r1 = x - p1.astype(F32)
    p2 = r1.astype(BF16)
    p3 = (r1 - p2.astype(F32)).astype(BF16)
    out = jnp.dot(tri, p1, preferred_element_type=F32)
    out = out + jnp.dot(tri, p2, preferred_element_type=F32)
    return out + jnp.dot(tri, p3, preferred_element_type=F32)


def _mlp_phases(z1_buf, g1_ref, b1_ref, w1_ref, w2_ref, g2_ref, b2_ref, act_buf, o_ref):
    x1 = _layer_norm_rows(z1_buf[...], g1_ref[...], b1_ref[...])
    xb = x1.astype(BF16)
    yield

    def act_of(hid):
        return jnp.square(jnp.maximum(hid, 0.0)).astype(BF16)

    n_up = D_FF // FF_CHUNK
    hid_prev = None
    for c in range(n_up):
        hid = jnp.dot(xb, w1_ref[:, c * FF_CHUNK:(c + 1) * FF_CHUNK],
                      preferred_element_type=F32)
        if hid_prev is not None:
            act_buf[:, (c - 1) * FF_CHUNK:c * FF_CHUNK] = act_of(hid_prev)
        hid_prev = hid
        yield
    act_buf[:, (n_up - 1) * FF_CHUNK:] = act_of(hid_prev)

    outs = []
    for j in range(D_MODEL // DOWN_GROUP):
        outs.append(jnp.dot(act_buf[...], w2_ref[:, j * DOWN_GROUP:(j + 1) * DOWN_GROUP],
                            preferred_element_type=F32))
        yield
    y = jnp.concatenate(outs, axis=1)
    o_ref[...] = _layer_norm_rows(ALPHA * x1 + y, g2_ref[...], b2_ref[...])
    yield


def _mixer_phases(x_ref, t_glob, wina_ref, winb_ref, bg_ref, wconv_ref, hng_ref, wpool_ref,
                  pscale_ref, wout_ref, tri_ref, bias_ref, z1_buf,
                  qk_ext, p_ext, vb_ref, o_buf, z_ref, zt_ref, ycat, ct_ref, m_ref,
                  *, ts, chunk):
    n_chunks = ts // chunk
    xb = x_ref[...].astype(BF16)

    qk_ext[CONV_HALO:CONV_HALO + ts, :] = jnp.dot(
        xb, wina_ref[:, OFF_Q:OFF_V], preferred_element_type=F32)
    yield
    vb_ref[...] = jnp.dot(xb, wina_ref[:, OFF_V:D_IN_A], preferred_element_type=F32).astype(BF16)
    u_go = jnp.dot(xb, winb_ref[:, :OFF_P], preferred_element_type=F32)
    o_buf[...] = u_go[:, OFF_O:]
    gates = u_go[:, :LANES] + bg_ref[...]
    yield
    p_ext[POOL_HALO:POOL_HALO + ts, :] = jnp.dot(
        xb, winb_ref[:, OFF_P:D_IN_B], preferred_element_type=F32)
    logf = jax.nn.log_sigmoid(gates)
    lane = lax.broadcasted_iota(jnp.int32, (chunk, LANES), 1)
    for c in range(n_chunks):
        rs = slice(c * chunk, (c + 1) * chunk)
        a_c = _cumsum_rows(tri_ref[...], logf[rs, :])
        z_c = jnp.where(lane < GATE_LANE + N_HEADS, gates[rs, :], a_c)
        z_ref[rs, :] = z_c
        zt_ref[:, rs] = z_c.T
    yield

    wc = wconv_ref[...]
    conv_tail = qk_ext[ts:ts + CONV_HALO, :]
    for c in reversed(range(n_chunks)):
        blk = qk_ext[c * chunk:c * chunk + chunk + CONV_HALO, :]
        y = wc[0:1, :] * pltpu.roll(blk, CONV_WIDTH - 1, 0)[CONV_HALO:, :]
        for j in range(1, CONV_WIDTH - 1):
            y = y + wc[j:j + 1, :] * pltpu.roll(blk, CONV_WIDTH - 1 - j, 0)[CONV_HALO:, :]
        y = y + wc[CONV_WIDTH - 1:CONV_WIDTH, :] * blk[CONV_HALO:, :]
        act = jax.nn.silu(y)
        rows = slice(CONV_HALO + c * chunk, CONV_HALO + (c + 1) * chunk)
        qk_ext[rows, OFF_Q:OFF_K] = act[:, OFF_Q:OFF_K]
        qk_ext[rows, OFF_K:OFF_V] = act[:, OFF_K:OFF_V] * (HEAD_DIM ** -0.5)
        yield
    qk_ext[0:CONV_HALO, :] = conv_tail

    ones_col = jnp.where(lane == 0, 1.0, 0.0).astype(BF16)
    for c in range(n_chunks):
        rs = slice(c * chunk, (c + 1) * chunk)
        rows = slice(CONV_HALO + c * chunk, CONV_HALO + (c + 1) * chunk)
        hd = [dict() for _ in range(N_HEADS)]
        for h, d in enumerate(hd):
            hs = slice(h * HEAD_DIM, (h + 1) * HEAD_DIM)
            d["q_b"] = qk_ext[rows, OFF_Q + h * HEAD_DIM:OFF_Q + (h + 1) * HEAD_DIM].astype(BF16)
            d["k_h"] = qk_ext[rows, OFF_K + h * HEAD_DIM:OFF_K + (h + 1) * HEAD_DIM]
            d["v_aug"] = jnp.concatenate([vb_ref[rs, hs], ones_col], axis=1)
            d["ct_prev"] = ct_ref[h]
            d["s"] = lax.dot_general(d["q_b"], d["k_h"].astype(BF16), (((1,), (1,)), ((), ())),
                                     preferred_element_type=F32)
        for h, d in enumerate(hd):
            li, la = GATE_LANE + h, GATE_LANE + N_HEADS + h
            icol = z_ref[rs, li:li + 1]
            acol = z_ref[rs, la:la + 1]
            irow = zt_ref[li:li + 1, rs]
            arow = zt_ref[la:la + 1, rs]
            gtot = acol[chunk - 1:chunk, :]
            m_prev = m_ref[h:h + 1, 0:1]
            w_row = gtot - arow + irow
            m_loc = jnp.max(w_row, axis=1, keepdims=True)
            e_col = jnp.exp(gtot - acol + icol - m_loc)
            d["m_new"] = jnp.maximum(gtot + m_prev, m_loc)
            d["s_old"] = jnp.exp(gtot + m_prev - d["m_new"])
            d["s_new"] = jnp.exp(m_loc - d["m_new"])
            d["ke_t"] = (e_col * d["k_h"]).T.astype(BF16)
            dlog = acol - arow + irow + bias_ref[...]
            m_inter = acol + m_prev
            d["m_t"] = jnp.maximum(m_inter, jnp.max(dlog, axis=1, keepdims=True))
            d["p"] = jnp.exp(dlog - d["m_t"])
            d["s_inter"] = jnp.exp(m_inter - d["m_t"])
        for h, d in enumerate(hd):
            ct_ref[h] = (d["s_old"] * d["ct_prev"]
                         + d["s_new"] * jnp.dot(d["ke_t"], d["v_aug"], preferred_element_type=F32))
            m_ref[h:h + 1, :] = jnp.broadcast_to(d["m_new"], (1, LANES))
        yield
        for d in hd:
            sp_b = (d["s"] * d["p"]).astype(BF16)
            d["tot"] = (jnp.dot(sp_b, d["v_aug"], preferred_element_type=F32)
                        + d["s_inter"] * jnp.dot(d["q_b"], d["ct_prev"].astype(BF16),
                                                 preferred_element_type=F32))
        for h, d in enumerate(hd):
            hs = slice(h * HEAD_DIM, (h + 1) * HEAD_DIM)
            num = d["tot"][:, :HEAD_DIM]
            den = d["tot"][:, HEAD_DIM:HEAD_DIM + 1]
            hh = num / jnp.maximum(jnp.abs(den), jnp.exp(-d["m_t"]))
            mu = jnp.mean(hh, axis=1, keepdims=True)
            hc = hh - mu
            var = jnp.mean(hc * hc, axis=1, keepdims=True)
            hn = hc * lax.rsqrt(var + LN_EPS) * hng_ref[:, hs]
            ycat[rs, hs] = (jax.nn.sigmoid(o_buf[rs, hs]) * hn).astype(BF16)
        yield

    ext = p_ext[...]
    p_ext[0:POOL_HALO, :] = ext[ts:ts + POOL_HALO, :]
    count = (t_glob * ts + 1 + lax.broadcasted_iota(jnp.int32, (ts, 1), 0)).astype(F32)
    wsum = ext
    shift = 1
    for gi, win in enumerate(POOL_WINDOWS):
        while shift < win:
            wsum = wsum + pltpu.roll(wsum, shift, 0)
            shift *= 2
        ls = slice(gi * POOL_GROUP_DIM, (gi + 1) * POOL_GROUP_DIM)
        cur = ext[POOL_HALO:, ls]
        d = wsum[POOL_HALO:, :POOL_GROUP_DIM] / jnp.minimum(count, float(win)) - cur
        yp = jnp.dot(d.astype(BF16), wpool_ref[gi], preferred_element_type=F32) * pscale_ref[:, ls]
        ycat[:, D_MLSTM + gi * POOL_GROUP_DIM:D_MLSTM + (gi + 1) * POOL_GROUP_DIM] = yp.astype(BF16)
        wsum = wsum[:, POOL_GROUP_DIM:]
        yield

    mix = jnp.dot(ycat[...], wout_ref[...], preferred_element_type=F32)
    z1_buf[...] = ALPHA * x_ref[...] + mix
    yield


def _layer_kernel(layer_ref, x_ref, wina_ref, winb_ref, bg_ref, wconv_ref, hng_ref, wpool_ref,
                  pscale_ref, g1_ref, b1_ref, g2_ref, b2_ref, wout_ref, w1_ref, w2_ref,
                  tri_ref, bias_ref,
                  nwout_ref, nw1_ref, nw2_ref,
                  o_ref, nwout_b_ref, nw1_b_ref, nw2_b_ref,
                  z1_buf, qk_ext, p_ext, vb_ref, o_buf, z_ref, zt_ref, ycat, ct_ref, m_ref,
                  act_buf, *, ts, chunk, nt, n_tiles):
    i = pl.program_id(0)
    t_glob = lax.rem(i, nt)

    @pl.when(t_glob == 0)
    def _():
        qk_ext[0:CONV_HALO, :] = jnp.zeros((CONV_HALO, 2 * D_MLSTM), F32)
        p_ext[0:POOL_HALO, :] = jnp.zeros((POOL_HALO, D_POOL), F32)
        ct_ref[...] = jnp.zeros_like(ct_ref)
        m_ref[...] = jnp.zeros_like(m_ref)

    nwout_b_ref[...] = nwout_ref[...].astype(BF16)
    nw1_b_ref[...] = nw1_ref[...].astype(BF16)
    nw2_b_ref[...] = nw2_ref[...].astype(BF16)

    def run(order):
        stages = {
            "X": _mixer_phases(
                x_ref, t_glob, wina_ref, winb_ref, bg_ref, wconv_ref, hng_ref, wpool_ref,
                pscale_ref, wout_ref, tri_ref, bias_ref, z1_buf, qk_ext, p_ext, vb_ref, o_buf,
                z_ref, zt_ref, ycat, ct_ref, m_ref, ts=ts, chunk=chunk),
            "M": _mlp_phases(z1_buf, g1_ref, b1_ref, w1_ref, w2_ref, g2_ref, b2_ref, act_buf,
                             o_ref),
        }
        for tag in order:
            next(stages[tag])
        done = object()
        for tag in set(order):
            assert next(stages[tag], done) is done

    pl.when(jnp.logical_and(i > 0, i < n_tiles))(lambda: run(PHASE_ORDER))
    pl.when(i == 0)(lambda: run(PHASE_ORDER.replace("M", "")))
    pl.when(i == n_tiles)(lambda: run(PHASE_ORDER.replace("X", "")))


def _stacked_spec(shape):
    return pl.BlockSpec((None,) + shape, lambda i, layer: (layer[0],) + (0,) * len(shape),
                        pipeline_mode=pl.Buffered(1))


def _whole_spec(shape):
    return pl.BlockSpec(shape, lambda i, layer: (0,) * len(shape),
                        pipeline_mode=pl.Buffered(1))


def _layer_call(x2d, stacked, wout_b, w1_b, w2_b, consts, next_f32, layer, *, seq):
    ts = SEQ_TILE
    chunk = MLSTM_CHUNK
    n_tiles = x2d.shape[0] // ts
    stacked_shapes = [
        (D_MODEL, D_IN_A), (D_MODEL, D_IN_B), (1, LANES), (CONV_WIDTH, 2 * D_MLSTM),
        (1, D_MLSTM), (len(POOL_WINDOWS), POOL_GROUP_DIM, POOL_GROUP_DIM), (1, D_POOL),
        (1, D_MODEL), (1, D_MODEL), (1, D_MODEL), (1, D_MODEL),
    ]
    x_spec = pl.BlockSpec((ts, D_MODEL), lambda i, layer: (jnp.minimum(i, n_tiles - 1), 0))
    o_spec = pl.BlockSpec((ts, D_MODEL), lambda i, layer: (jnp.maximum(i - 1, 0), 0))

    def slab_in(rows, cols):
        return pl.BlockSpec(
            (None, rows // n_tiles, cols),
            lambda i, layer: (jnp.minimum(layer[0] + 1, DEPTH - 1), jnp.minimum(i, n_tiles - 1), 0))

    def slab_out(rows, cols):
        return pl.BlockSpec((rows // n_tiles, cols),
                            lambda i, layer: (jnp.minimum(i, n_tiles - 1), 0))

    big = [(D_MODEL, D_MODEL), (D_MODEL, D_FF), (D_FF, D_MODEL)]
    grid_spec = pltpu.PrefetchScalarGridSpec(
        num_scalar_prefetch=1,
        grid=(n_tiles + 1,),
        in_specs=([x_spec] + [_stacked_spec(s) for s in stacked_shapes]
                  + [_whole_spec(s) for s in big]
                  + [_whole_spec((chunk, chunk)), _whole_spec((chunk, chunk))]
                  + [slab_in(*s) for s in big]),
        out_specs=[o_spec] + [slab_out(*s) for s in big],
        scratch_shapes=[
            pltpu.VMEM((ts, D_MODEL), F32),
            pltpu.VMEM((ts + CONV_HALO, 2 * D_MLSTM), F32),
            pltpu.VMEM((ts + POOL_HALO, D_POOL), F32),
            pltpu.VMEM((ts, D_MLSTM), BF16),
            pltpu.VMEM((ts, D_MLSTM), F32),
            pltpu.VMEM((ts, LANES), F32),
            pltpu.VMEM((LANES, ts), F32),
            pltpu.VMEM((ts, D_MODEL), BF16),
            pltpu.VMEM((N_HEADS, HEAD_DIM, 2 * HEAD_DIM), F32),
            pltpu.VMEM((8, LANES), F32),
            pltpu.VMEM((ts, D_FF), BF16),
        ])
    return pl.pallas_call(
        functools.partial(_layer_kernel, ts=ts, chunk=chunk, nt=seq // ts, n_tiles=n_tiles),
        grid_spec=grid_spec,
        out_shape=[jax.ShapeDtypeStruct(x2d.shape, F32)]
        + [jax.ShapeDtypeStruct(s, BF16) for s in big],
        compiler_params=pltpu.CompilerParams(
            dimension_semantics=("arbitrary",),
            vmem_limit_bytes=VMEM_LIMIT_BYTES),
        name="layer",
    )(jnp.full((1,), layer, jnp.int32), x2d, *stacked, wout_b, w1_b, w2_b, *consts, *next_f32)


def kernel(x, w_in, b_gate, w_conv, hn_g, w_pool, pool_scale, w_out,
           ln1_g, ln1_b, w_ff1, w_ff2, ln2_g, ln2_b):
    batch, seq, d_model = x.shape
    assert d_model == D_MODEL and seq % SEQ_TILE == 0 and SEQ_TILE % MLSTM_CHUNK == 0
    win_a = w_in[:, :, :D_IN_A].astype(BF16)
    win_b = jnp.pad(w_in[:, :, D_IN_A:].astype(BF16), ((0, 0), (0, 0), (GATE_LANE, 0)))
    stacked = (
        win_a, win_b,
        jnp.pad(b_gate, ((0, 0), (GATE_LANE, 0)))[:, None, :],
        w_conv, hn_g[:, None, :], w_pool.astype(BF16), pool_scale[:, None, :],
        ln1_g[:, None, :], ln1_b[:, None, :], ln2_g[:, None, :], ln2_b[:, None, :],
    )
    causal = jnp.tril(jnp.ones((MLSTM_CHUNK, MLSTM_CHUNK), dtype=bool))
    consts = (causal.astype(BF16), jnp.where(causal, 0.0, -jnp.inf).astype(F32))
    next_f32 = (w_out, w_ff1, w_ff2)
    big_b = (w_out[0].astype(BF16), w_ff1[0].astype(BF16), w_ff2[0].astype(BF16))
    h = x.reshape(batch * seq, d_model)
    for layer in range(DEPTH):
        h, *big_b = _layer_call(h, stacked, *big_b, consts, next_f32, layer, seq=seq)
    return h.reshape(batch, seq, d_model)
```

```python
import functools

import jax
import jax.numpy as jnp
from jax import lax
from jax.experimental import pallas as pl
from jax.experimental.pallas import tpu as pltpu

D_MODEL = 1024
DEPTH = 4
N_HEADS = 4
HEAD_DIM = 128
D_MLSTM = N_HEADS * HEAD_DIM
POOL_WINDOWS = (2, 4, 8, 16)
POOL_GROUP_DIM = 128
D_POOL = len(POOL_WINDOWS) * POOL_GROUP_DIM
CONV_WIDTH = 4
D_FF = 4 * D_MODEL
ALPHA = (2.0 * DEPTH) ** 0.25
LN_EPS = 1e-5

LANES = 128
N_GATES = 2 * N_HEADS
OFF_Q = 0
OFF_K = OFF_Q + D_MLSTM
OFF_V = OFF_K + D_MLSTM
D_IN_A = OFF_V + D_MLSTM
GATE_LANE = LANES - N_GATES
OFF_O = LANES
OFF_P = OFF_O + D_MLSTM
D_IN_B = OFF_P + D_POOL

CONV_HALO = 8
POOL_HALO = 16

SEQ_TILE = 512
MLSTM_CHUNK = 256
FF_CHUNK = 1024
DOWN_GROUP = 256
VMEM_LIMIT_BYTES = 60000 * 1024

F32 = jnp.float32
BF16 = jnp.bfloat16

PHASE_ORDER = "XMM" + "XXM" + "XM" * 2 + "XXM" * 4 + "XXXXX" + "M"


def _layer_norm_rows(z, g, b):
    mu = jnp.mean(z, axis=-1, keepdims=True)
    zc = z - mu
    var = jnp.mean(zc * zc, axis=-1, keepdims=True)
    return zc * lax.rsqrt(var + LN_EPS) * g + b


def _cumsum_rows(tri, x):
    p1 = x.astype(BF16)
    r1 = x - p1.astype(F32)
    p2 = r1.astype(BF16)
    p3 = (r1 - p2.astype(F32)).astype(BF16)
    out = jnp.dot(tri, p1, preferred_element_type=F32)
    out = out + jnp.dot(tri, p2, preferred_element_type=F32)
    return out + jnp.dot(tri, p3, preferred_element_type=F32)


def _mlp_phases(z1_buf, g1_ref, b1_ref, w1_ref, w2_ref, g2_ref, b2_ref, act_buf, o_ref):
    x1 = _layer_norm_rows(z1_buf[...], g1_ref[...], b1_ref[...])
    xb = x1.astype(BF16)
    yield

    def act_of(hid):
        return jnp.square(jnp.maximum(hid, 0.0)).astype(BF16)

    n_up = D_FF // FF_CHUNK
    hid_prev = None
    for c in range(n_up):
        hid = jnp.dot(xb, w1_ref[:, c * FF_CHUNK:(c + 1) * FF_CHUNK],
                      preferred_element_type=F32)
        if hid_prev is not None:
            act_buf[:, (c - 1) * FF_CHUNK:c * FF_CHUNK] = act_of(hid_prev)
        hid_prev = hid
        yield
    act_buf[:, (n_up - 1) * FF_CHUNK:] = act_of(hid_prev)

    outs = []
    for j in range(D_MODEL // DOWN_GROUP):
        outs.append(jnp.dot(act_buf[...], w2_ref[:, j * DOWN_GROUP:(j + 1) * DOWN_GROUP],
                            preferred_element_type=F32))
        yield
    y = jnp.concatenate(outs, axis=1)
    o_ref[...] = _layer_norm_rows(ALPHA * x1 + y, g2_ref[...], b2_ref[...])
    yield


def _mixer_phases(x_ref, t_glob, wina_ref, winb_ref, bg_ref, wconv_ref, hng_ref, wpool_ref,
                  pscale_ref, wout_ref, tri_ref, bias_ref, z1_buf,
                  qk_ext, p_ext, vb_ref, o_buf, z_ref, zt_ref, ycat, ct_ref, m_ref,
                  *, ts, chunk):
    n_chunks = ts // chunk
    xb = x_ref[...].astype(BF16)

    qk_ext[CONV_HALO:CONV_HALO + ts, :] = jnp.dot(
        xb, wina_ref[:, OFF_Q:OFF_V], preferred_element_type=F32)
    yield
    vb_ref[...] = jnp.dot(xb, wina_ref[:, OFF_V:D_IN_A], preferred_element_type=F32).astype(BF16)
    u_go = jnp.dot(xb, winb_ref[:, :OFF_P], preferred_element_type=F32)
    o_buf[...] = u_go[:, OFF_O:]
    gates = u_go[:, :LANES] + bg_ref[...]
    yield
    p_ext[POOL_HALO:POOL_HALO + ts, :] = jnp.dot(
        xb, winb_ref[:, OFF_P:D_IN_B], preferred_element_type=F32)
    logf = jax.nn.log_sigmoid(gates)
    lane = lax.broadcasted_iota(jnp.int32, (chunk, LANES), 1)
    for c in range(n_chunks):
        rs = slice(c * chunk, (c + 1) * chunk)
        a_c = _cumsum_rows(tri_ref[...], logf[rs, :])
        z_c = jnp.where(lane < GATE_LANE + N_HEADS, gates[rs, :], a_c)
        z_ref[rs, :] = z_c
        zt_ref[:, rs] = z_c.T
    yield

    wc = wconv_ref[...]
    conv_tail = qk_ext[ts:ts + CONV_HALO, :]
    for c in reversed(range(n_chunks)):
        blk = qk_ext[c * chunk:c * chunk + chunk + CONV_HALO, :]
        y = wc[0:1, :] * pltpu.roll(blk, CONV_WIDTH - 1, 0)[CONV_HALO:, :]
        for j in range(1, CONV_WIDTH - 1):
            y = y + wc[j:j + 1, :] * pltpu.roll(blk, CONV_WIDTH - 1 - j, 0)[CONV_HALO:, :]
        y = y + wc[CONV_WIDTH - 1:CONV_WIDTH, :] * blk[CONV_HALO:, :]
        act = jax.nn.silu(y)
        rows = slice(CONV_HALO + c * chunk, CONV_HALO + (c + 1) * chunk)
        qk_ext[rows, OFF_Q:OFF_K] = act[:, OFF_Q:OFF_K]
        qk_ext[rows, OFF_K:OFF_V] = act[:, OFF_K:OFF_V] * (HEAD_DIM ** -0.5)
        yield
    qk_ext[0:CONV_HALO, :] = conv_tail

    ones_col = jnp.where(lane == 0, 1.0, 0.0).astype(BF16)
    for c in range(n_chunks):
        rs = slice(c * chunk, (c + 1) * chunk)
        rows = slice(CONV_HALO + c * chunk, CONV_HALO + (c + 1) * chunk)
        for h in range(N_HEADS):
            hs = slice(h * HEAD_DIM, (h + 1) * HEAD_DIM)
            li, la = GATE_LANE + h, GATE_LANE + N_HEADS + h
            icol = z_ref[rs, li:li + 1]
            acol = z_ref[rs, la:la + 1]
            irow = zt_ref[li:li + 1, rs]
            arow = zt_ref[la:la + 1, rs]
            gtot = acol[chunk - 1:chunk, :]
            m_prev = m_ref[h:h + 1, 0:1]
            ct_prev = ct_ref[h]

            q_b = qk_ext[rows, OFF_Q + h * HEAD_DIM:OFF_Q + (h + 1) * HEAD_DIM].astype(BF16)
            k_h = qk_ext[rows, OFF_K + h * HEAD_DIM:OFF_K + (h + 1) * HEAD_DIM]
            k_b = k_h.astype(BF16)
            v_aug = jnp.concatenate([vb_ref[rs, hs], ones_col], axis=1)

            w_row = gtot - arow + irow
            m_loc = jnp.max(w_row, axis=1, keepdims=True)
            e_col = jnp.exp(gtot - acol + icol - m_loc)

            dlog = acol - arow + irow + bias_ref[...]
            m_inter = acol + m_prev
            m_t = jnp.maximum(m_inter, jnp.max(dlog, axis=1, keepdims=True))
            p = jnp.exp(dlog - m_t)
            s_inter = jnp.exp(m_inter - m_t)
            s = lax.dot_general(q_b, k_b, (((1,), (1,)), ((), ())), preferred_element_type=F32)
            sp_b = (s * p).astype(BF16)
            tot = (jnp.dot(sp_b, v_aug, preferred_element_type=F32)
                   + s_inter * jnp.dot(q_b, ct_prev.astype(BF16), preferred_element_type=F32))
            num = tot[:, :HEAD_DIM]
            den = tot[:, HEAD_DIM:HEAD_DIM + 1]
            hh = num / jnp.maximum(jnp.abs(den), jnp.exp(-m_t))

            mu = jnp.mean(hh, axis=1, keepdims=True)
            hc = hh - mu
            var = jnp.mean(hc * hc, axis=1, keepdims=True)
            hn = hc * lax.rsqrt(var + LN_EPS) * hng_ref[:, hs]
            ycat[rs, hs] = (jax.nn.sigmoid(o_buf[rs, hs]) * hn).astype(BF16)

            m_new = jnp.maximum(gtot + m_prev, m_loc)
            s_old = jnp.exp(gtot + m_prev - m_new)
            s_new = jnp.exp(m_loc - m_new)
            ke_t = (e_col * k_h).T.astype(BF16)
            ct_ref[h] = s_old * ct_prev + s_new * jnp.dot(ke_t, v_aug, preferred_element_type=F32)
            m_ref[h:h + 1, :] = jnp.broadcast_to(m_new, (1, LANES))
            yield

    ext = p_ext[...]
    p_ext[0:POOL_HALO, :] = ext[ts:ts + POOL_HALO, :]
    count = (t_glob * ts + 1 + lax.broadcasted_iota(jnp.int32, (ts, 1), 0)).astype(F32)
    wsum = ext
    shift = 1
    for gi, win in enumerate(POOL_WINDOWS):
        while shift < win:
            wsum = wsum + pltpu.roll(wsum, shift, 0)
            shift *= 2
        ls = slice(gi * POOL_GROUP_DIM, (gi + 1) * POOL_GROUP_DIM)
        cur = ext[POOL_HALO:, ls]
        d = wsum[POOL_HALO:, :POOL_GROUP_DIM] / jnp.minimum(count, float(win)) - cur
        yp = jnp.dot(d.astype(BF16), wpool_ref[gi], preferred_element_type=F32) * pscale_ref[:, ls]
        ycat[:, D_MLSTM + gi * POOL_GROUP_DIM:D_MLSTM + (gi + 1) * POOL_GROUP_DIM] = yp.astype(BF16)
        wsum = wsum[:, POOL_GROUP_DIM:]
        yield

    mix = jnp.dot(ycat[...], wout_ref[...], preferred_element_type=F32)
    z1_buf[...] = ALPHA * x_ref[...] + mix
    yield


def _layer_kernel(layer_ref, x_ref, wina_ref, winb_ref, bg_ref, wconv_ref, hng_ref, wpool_ref,
                  pscale_ref, g1_ref, b1_ref, g2_ref, b2_ref, wout_ref, w1_ref, w2_ref,
                  tri_ref, bias_ref,
                  nwout_ref, nw1_ref, nw2_ref,
                  o_ref, nwout_b_ref, nw1_b_ref, nw2_b_ref,
                  z1_buf, qk_ext, p_ext, vb_ref, o_buf, z_ref, zt_ref, ycat, ct_ref, m_ref,
                  act_buf, *, ts, chunk, nt, n_tiles):
    i = pl.program_id(0)
    t_glob = lax.rem(i, nt)

    @pl.when(t_glob == 0)
    def _():
        qk_ext[0:CONV_HALO, :] = jnp.zeros((CONV_HALO, 2 * D_MLSTM), F32)
        p_ext[0:POOL_HALO, :] = jnp.zeros((POOL_HALO, D_POOL), F32)
        ct_ref[...] = jnp.zeros_like(ct_ref)
        m_ref[...] = jnp.zeros_like(m_ref)

    nwout_b_ref[...] = nwout_ref[...].astype(BF16)
    nw1_b_ref[...] = nw1_ref[...].astype(BF16)
    nw2_b_ref[...] = nw2_ref[...].astype(BF16)

    def run(order):
        stages = {
            "X": _mixer_phases(
                x_ref, t_glob, wina_ref, winb_ref, bg_ref, wconv_ref, hng_ref, wpool_ref,
                pscale_ref, wout_ref, tri_ref, bias_ref, z1_buf, qk_ext, p_ext, vb_ref, o_buf,
                z_ref, zt_ref, ycat, ct_ref, m_ref, ts=ts, chunk=chunk),
            "M": _mlp_phases(z1_buf, g1_ref, b1_ref, w1_ref, w2_ref, g2_ref, b2_ref, act_buf,
                             o_ref),
        }
        for tag in order:
            next(stages[tag])
        done = object()
        for tag in set(order):
            assert next(stages[tag], done) is done

    pl.when(jnp.logical_and(i > 0, i < n_tiles))(lambda: run(PHASE_ORDER))
    pl.when(i == 0)(lambda: run(PHASE_ORDER.replace("M", "")))
    pl.when(i == n_tiles)(lambda: run(PHASE_ORDER.replace("X", "")))


def _stacked_spec(shape):
    return pl.BlockSpec((None,) + shape, lambda i, layer: (layer[0],) + (0,) * len(shape),
                        pipeline_mode=pl.Buffered(1))


def _whole_spec(shape):
    return pl.BlockSpec(shape, lambda i, layer: (0,) * len(shape),
                        pipeline_mode=pl.Buffered(1))


def _layer_call(x2d, stacked, wout_b, w1_b, w2_b, consts, next_f32, layer, *, seq):
    ts = SEQ_TILE
    chunk = MLSTM_CHUNK
    n_tiles = x2d.shape[0] // ts
    stacked_shapes = [
        (D_MODEL, D_IN_A), (D_MODEL, D_IN_B), (1, LANES), (CONV_WIDTH, 2 * D_MLSTM),
        (1, D_MLSTM), (len(POOL_WINDOWS), POOL_GROUP_DIM, POOL_GROUP_DIM), (1, D_POOL),
        (1, D_MODEL), (1, D_MODEL), (1, D_MODEL), (1, D_MODEL),
    ]
    x_spec = pl.BlockSpec((ts, D_MODEL), lambda i, layer: (jnp.minimum(i, n_tiles - 1), 0))
    o_spec = pl.BlockSpec((ts, D_MODEL), lambda i, layer: (jnp.maximum(i - 1, 0), 0))

    def slab_in(rows, cols):
        return pl.BlockSpec(
            (None, rows // n_tiles, cols),
            lambda i, layer: (jnp.minimum(layer[0] + 1, DEPTH - 1), jnp.minimum(i, n_tiles - 1), 0))

    def slab_out(rows, cols):
        return pl.BlockSpec((rows // n_tiles, cols),
                            lambda i, layer: (jnp.minimum(i, n_tiles - 1), 0))

    big = [(D_MODEL, D_MODEL), (D_MODEL, D_FF), (D_FF, D_MODEL)]
    grid_spec = pltpu.PrefetchScalarGridSpec(
        num_scalar_prefetch=1,
        grid=(n_tiles + 1,),
        in_specs=([x_spec] + [_stacked_spec(s) for s in stacked_shapes]
                  + [_whole_spec(s) for s in big]
                  + [_whole_spec((chunk, chunk)), _whole_spec((chunk, chunk))]
                  + [slab_in(*s) for s in big]),
        out_specs=[o_spec] + [slab_out(*s) for s in big],
        scratch_shapes=[
            pltpu.VMEM((ts, D_MODEL), F32),
            pltpu.VMEM((ts + CONV_HALO, 2 * D_MLSTM), F32),
            pltpu.VMEM((ts + POOL_HALO, D_POOL), F32),
            pltpu.VMEM((ts, D_MLSTM), BF16),
            pltpu.VMEM((ts, D_MLSTM), F32),
            pltpu.VMEM((ts, LANES), F32),
            pltpu.VMEM((LANES, ts), F32),
            pltpu.VMEM((ts, D_MODEL), BF16),
            pltpu.VMEM((N_HEADS, HEAD_DIM, 2 * HEAD_DIM), F32),
            pltpu.VMEM((8, LANES), F32),
            pltpu.VMEM((ts, D_FF), BF16),
        ])
    return pl.pallas_call(
        functools.partial(_layer_kernel, ts=ts, chunk=chunk, nt=seq // ts, n_tiles=n_tiles),
        grid_spec=grid_spec,
        out_shape=[jax.ShapeDtypeStruct(x2d.shape, F32)]
        + [jax.ShapeDtypeStruct(s, BF16) for s in big],
        compiler_params=pltpu.CompilerParams(
            dimension_semantics=("arbitrary",),
            vmem_limit_bytes=VMEM_LIMIT_BYTES),
        name="layer",
    )(jnp.full((1,), layer, jnp.int32), x2d, *stacked, wout_b, w1_b, w2_b, *consts, *next_f32)


def kernel(x, w_in, b_gate, w_conv, hn_g, w_pool, pool_scale, w_out,
           ln1_g, ln1_b, w_ff1, w_ff2, ln2_g, ln2_b):
    batch, seq, d_model = x.shape
    assert d_model == D_MODEL and seq % SEQ_TILE == 0 and SEQ_TILE % MLSTM_CHUNK == 0
    assert D_MODEL % (16 * (batch * seq // SEQ_TILE)) == 0
    win_a = w_in[:, :, :D_IN_A].astype(BF16)
    win_b = jnp.pad(w_in[:, :, D_IN_A:].astype(BF16), ((0, 0), (0, 0), (GATE_LANE, 0)))
    stacked = (
        win_a, win_b,
        jnp.pad(b_gate, ((0, 0), (GATE_LANE, 0)))[:, None, :],
        w_conv, hn_g[:, None, :], w_pool.astype(BF16), pool_scale[:, None, :],
        ln1_g[:, None, :], ln1_b[:, None, :], ln2_g[:, None, :], ln2_b[:, None, :],
    )
    causal = jnp.tril(jnp.ones((MLSTM_CHUNK, MLSTM_CHUNK), dtype=bool))
    consts = (causal.astype(BF16), jnp.where(causal, 0.0, -jnp.inf).astype(F32))
    next_f32 = (w_out, w_ff1, w_ff2)
    big_b = (w_out[0].astype(BF16), w_ff1[0].astype(BF16), w_ff2[0].astype(BF16))
    h = x.reshape(batch * seq, d_model)
    for layer in range(DEPTH):
        h, *big_b = _layer_call(h, stacked, *big_b, consts, next_f32, layer, seq=seq)
    return h.reshape(batch, seq, d_model)
```

```python
import functools

import jax
import jax.numpy as jnp
from jax import lax
from jax.experimental import pallas as pl
from jax.experimental.pallas import tpu as pltpu

D_MODEL = 1024
DEPTH = 4
N_HEADS = 4
HEAD_DIM = 128
D_MLSTM = N_HEADS * HEAD_DIM
POOL_WINDOWS = (2, 4, 8, 16)
POOL_GROUP_DIM = 128
D_POOL = len(POOL_WINDOWS) * POOL_GROUP_DIM
CONV_WIDTH = 4
D_FF = 4 * D_MODEL
ALPHA = (2.0 * DEPTH) ** 0.25
LN_EPS = 1e-5

LANES = 128
N_GATES = 2 * N_HEADS
OFF_Q = 0
OFF_K = OFF_Q + D_MLSTM
OFF_V = OFF_K + D_MLSTM
D_IN_A = OFF_V + D_MLSTM
GATE_LANE = LANES - N_GATES
OFF_O = LANES
OFF_P = OFF_O + D_MLSTM
D_IN_B = OFF_P + D_POOL

CONV_HALO = 8
POOL_HALO = 16

SEQ_TILE = 512
MLSTM_CHUNK = 256
FF_CHUNK = 1024
DOWN_GROUP = 256
VMEM_LIMIT_BYTES = 60000 * 1024

F32 = jnp.float32
BF16 = jnp.bfloat16

PHASE_ORDER = "XMM" + "XXM" + "XM" * 2 + "XXM" * 4 + "XXXXX" + "M"


def _layer_norm_rows(z, g, b):
    mu = jnp.mean(z, axis=-1, keepdims=True)
    zc = z - mu
    var = jnp.mean(zc * zc, axis=-1, keepdims=True)
    return zc * lax.rsqrt(var + LN_EPS) * g + b


def _cumsum_rows(tri, x):
    p1 = x.astype(BF16)
    r1 = x - p1.astype(F32)
    p2 = r1.astype(BF16)
    p3 = (r1 - p2.astype(F32)).astype(BF16)
    out = jnp.dot(tri, p1, preferred_element_type=F32)
    out = out + jnp.dot(tri, p2, preferred_element_type=F32)
    return out + jnp.dot(tri, p3, preferred_element_type=F32)


def _mlp_phases(z1_buf, g1_ref, b1_ref, w1_ref, w2_ref, g2_ref, b2_ref, act_buf, o_ref):
    x1 = _layer_norm_rows(z1_buf[...], g1_ref[...], b1_ref[...])
    xb = x1.astype(BF16)
    yield

    def act_of(hid):
        return jnp.square(jnp.maximum(hid, 0.0)).astype(BF16)

    n_up = D_FF // FF_CHUNK
    hid_prev = None
    for c in range(n_up):
        hid = jnp.dot(xb, w1_ref[:, c * FF_CHUNK:(c + 1) * FF_CHUNK],
                      preferred_element_type=F32)
        if hid_prev is not None:
            act_buf[:, (c - 1) * FF_CHUNK:c * FF_CHUNK] = act_of(hid_prev)
        hid_prev = hid
        yield
    act_buf[:, (n_up - 1) * FF_CHUNK:] = act_of(hid_prev)

    outs = []
    for j in range(D_MODEL // DOWN_GROUP):
        outs.append(jnp.dot(act_buf[...], w2_ref[:, j * DOWN_GROUP:(j + 1) * DOWN_GROUP],
                            preferred_element_type=F32))
        yield
    y = jnp.concatenate(outs, axis=1)
    o_ref[...] = _layer_norm_rows(ALPHA * x1 + y, g2_ref[...], b2_ref[...])
    yield


def _mixer_phases(x_ref, t_glob, wina_ref, winb_ref, bg_ref, wconv_ref, hng_ref, wpool_ref,
                  pscale_ref, wout_ref, tri_ref, bias_ref, z1_buf,
                  qk_ext, p_ext, vb_ref, o_buf, z_ref, zt_ref, ycat, ct_ref, m_ref,
                  *, ts, chunk):
    n_chunks = ts // chunk
    xb = x_ref[...].astype(BF16)

    qk_ext[CONV_HALO:CONV_HALO + ts, :] = jnp.dot(
        xb, wina_ref[:, OFF_Q:OFF_V], preferred_element_type=F32)
    yield
    vb_ref[...] = jnp.dot(xb, wina_ref[:, OFF_V:D_IN_A], preferred_element_type=F32).astype(BF16)
    u_go = jnp.dot(xb, winb_ref[:, :OFF_P], preferred_element_type=F32)
    o_buf[...] = u_go[:, OFF_O:]
    gates = u_go[:, :LANES] + bg_ref[...]
    yield
    p_ext[POOL_HALO:POOL_HALO + ts, :] = jnp.dot(
        xb, winb_ref[:, OFF_P:D_IN_B], preferred_element_type=F32)
    logf = jax.nn.log_sigmoid(gates)
    lane = lax.broadcasted_iota(jnp.int32, (chunk, LANES), 1)
    for c in range(n_chunks):
        rs = slice(c * chunk, (c + 1) * chunk)
        a_c = _cumsum_rows(tri_ref[...], logf[rs, :])
        z_c = jnp.where(lane < GATE_LANE + N_HEADS, gates[rs, :], a_c)
        z_ref[rs, :] = z_c
        zt_ref[:, rs] = z_c.T
    yield

    wc = wconv_ref[...]
    conv_tail = qk_ext[ts:ts + CONV_HALO, :]
    for c in reversed(range(n_chunks)):
        blk = qk_ext[c * chunk:c * chunk + chunk + CONV_HALO, :]
        y = wc[0:1, :] * pltpu.roll(blk, CONV_WIDTH - 1, 0)[CONV_HALO:, :]
        for j in range(1, CONV_WIDTH - 1):
            y = y + wc[j:j + 1, :] * pltpu.roll(blk, CONV_WIDTH - 1 - j, 0)[CONV_HALO:, :]
        y = y + wc[CONV_WIDTH - 1:CONV_WIDTH, :] * blk[CONV_HALO:, :]
        act = jax.nn.silu(y)
        rows = slice(CONV_HALO + c * chunk, CONV_HALO + (c + 1) * chunk)
        qk_ext[rows, OFF_Q:OFF_K] = act[:, OFF_Q:OFF_K]
        qk_ext[rows, OFF_K:OFF_V] = act[:, OFF_K:OFF_V] * (HEAD_DIM ** -0.5)
        yield
    qk_ext[0:CONV_HALO, :] = conv_tail

    ones_col = jnp.where(lane == 0, 1.0, 0.0).astype(BF16)
    for c in range(n_chunks):
        rs = slice(c * chunk, (c + 1) * chunk)
        rows = slice(CONV_HALO + c * chunk, CONV_HALO + (c + 1) * chunk)
        for h in range(N_HEADS):
            hs = slice(h * HEAD_DIM, (h + 1) * HEAD_DIM)
            li, la = GATE_LANE + h, GATE_LANE + N_HEADS + h
            icol = z_ref[rs, li:li + 1]
            acol = z_ref[rs, la:la + 1]
            irow = zt_ref[li:li + 1, rs]
            arow = zt_ref[la:la + 1, rs]
            gtot = acol[chunk - 1:chunk, :]
            m_prev = m_ref[h:h + 1, 0:1]
            ct_prev = ct_ref[h]

            q_b = qk_ext[rows, OFF_Q + h * HEAD_DIM:OFF_Q + (h + 1) * HEAD_DIM].astype(BF16)
            k_h = qk_ext[rows, OFF_K + h * HEAD_DIM:OFF_K + (h + 1) * HEAD_DIM]
            k_b = k_h.astype(BF16)
            v_aug = jnp.concatenate([vb_ref[rs, hs], ones_col], axis=1)

            w_row = gtot - arow + irow
            m_loc = jnp.max(w_row, axis=1, keepdims=True)
            e_col = jnp.exp(gtot - acol + icol - m_loc)

            dlog = acol - arow + irow + bias_ref[...]
            m_inter = acol + m_prev
            m_t = jnp.maximum(m_inter, jnp.max(dlog, axis=1, keepdims=True))
            p = jnp.exp(dlog - m_t)
            s_inter = jnp.exp(m_inter - m_t)
            s = lax.dot_general(q_b, k_b, (((1,), (1,)), ((), ())), preferred_element_type=F32)
            sp_b = (s * p).astype(BF16)
            tot = (jnp.dot(sp_b, v_aug, preferred_element_type=F32)
                   + s_inter * jnp.dot(q_b, ct_prev.astype(BF16), preferred_element_type=F32))
            num = tot[:, :HEAD_DIM]
            den = tot[:, HEAD_DIM:HEAD_DIM + 1]
            hh = num / jnp.maximum(jnp.abs(den), jnp.exp(-m_t))

            mu = jnp.mean(hh, axis=1, keepdims=True)
            hc = hh - mu
            var = jnp.mean(hc * hc, axis=1, keepdims=True)
            hn = hc * lax.rsqrt(var + LN_EPS) * hng_ref[:, hs]
            ycat[rs, hs] = (jax.nn.sigmoid(o_buf[rs, hs]) * hn).astype(BF16)

            m_new = jnp.maximum(gtot + m_prev, m_loc)
            s_old = jnp.exp(gtot + m_prev - m_new)
            s_new = jnp.exp(m_loc - m_new)
            ke_t = (e_col * k_h).T.astype(BF16)
            ct_ref[h] = s_old * ct_prev + s_new * jnp.dot(ke_t, v_aug, preferred_element_type=F32)
            m_ref[h:h + 1, :] = jnp.broadcast_to(m_new, (1, LANES))
            yield

    ext = p_ext[...]
    p_ext[0:POOL_HALO, :] = ext[ts:ts + POOL_HALO, :]
    count = (t_glob * ts + 1 + lax.broadcasted_iota(jnp.int32, (ts, 1), 0)).astype(F32)
    wsum = ext
    shift = 1
    for gi, win in enumerate(POOL_WINDOWS):
        while shift < win:
            wsum = wsum + pltpu.roll(wsum, shift, 0)
            shift *= 2
        ls = slice(gi * POOL_GROUP_DIM, (gi + 1) * POOL_GROUP_DIM)
        cur = ext[POOL_HALO:, ls]
        d = wsum[POOL_HALO:, :POOL_GROUP_DIM] / jnp.minimum(count, float(win)) - cur
        yp = jnp.dot(d.astype(BF16), wpool_ref[gi], preferred_element_type=F32) * pscale_ref[:, ls]
        ycat[:, D_MLSTM + gi * POOL_GROUP_DIM:D_MLSTM + (gi + 1) * POOL_GROUP_DIM] = yp.astype(BF16)
        wsum = wsum[:, POOL_GROUP_DIM:]
        yield

    mix = jnp.dot(ycat[...], wout_ref[...], preferred_element_type=F32)
    z1_buf[...] = ALPHA * x_ref[...] + mix
    yield


def _layer_kernel(layer_ref, x_ref, wina_ref, winb_ref, bg_ref, wconv_ref, hng_ref, wpool_ref,
                  pscale_ref, g1_ref, b1_ref, g2_ref, b2_ref, wout_ref, w1_ref, w2_ref,
                  tri_ref, bias_ref,
                  nwout_ref, nw1_ref, nw2_ref,
                  o_ref, nwout_b_ref, nw1_b_ref, nw2_b_ref,
                  z1_buf, qk_ext, p_ext, vb_ref, o_buf, z_ref, zt_ref, ycat, ct_ref, m_ref,
                  act_buf, *, ts, chunk, nt, n_tiles):
    i = pl.program_id(0)
    t_glob = lax.rem(i, nt)

    @pl.when(t_glob == 0)
    def _():
        qk_ext[0:CONV_HALO, :] = jnp.zeros((CONV_HALO, 2 * D_MLSTM), F32)
        p_ext[0:POOL_HALO, :] = jnp.zeros((POOL_HALO, D_POOL), F32)
        ct_ref[...] = jnp.zeros_like(ct_ref)
        m_ref[...] = jnp.zeros_like(m_ref)

    nwout_b_ref[...] = nwout_ref[...].astype(BF16)
    nw1_b_ref[...] = nw1_ref[...].astype(BF16)
    nw2_b_ref[...] = nw2_ref[...].astype(BF16)

    def run(order):
        stages = {
            "X": _mixer_phases(
                x_ref, t_glob, wina_ref, winb_ref, bg_ref, wconv_ref, hng_ref, wpool_ref,
                pscale_ref, wout_ref, tri_ref, bias_ref, z1_buf, qk_ext, p_ext, vb_ref, o_buf,
                z_ref, zt_ref, ycat, ct_ref, m_ref, ts=ts, chunk=chunk),
            "M": _mlp_phases(z1_buf, g1_ref, b1_ref, w1_ref, w2_ref, g2_ref, b2_ref, act_buf,
                             o_ref),
        }
        for tag in order:
            next(stages[tag])
        done = object()
        for tag in set(order):
            assert next(stages[tag], done) is done

    pl.when(jnp.logical_and(i > 0, i < n_tiles))(lambda: run(PHASE_ORDER))
    pl.when(i == 0)(lambda: run(PHASE_ORDER.replace("M", "")))
    pl.when(i == n_tiles)(lambda: run(PHASE_ORDER.replace("X", "")))


def _stacked_spec(shape):
    return pl.BlockSpec((None,) + shape, lambda i, layer: (layer[0],) + (0,) * len(shape),
                        pipeline_mode=pl.Buffered(1))


def _whole_spec(shape):
    return pl.BlockSpec(shape, lambda i, layer: (0,) * len(shape),
                        pipeline_mode=pl.Buffered(1))


def _layer_call(x2d, stacked, wout_b, w1_b, w2_b, consts, next_f32, layer, *, seq):
    ts = SEQ_TILE
    chunk = MLSTM_CHUNK
    n_tiles = x2d.shape[0] // ts
    stacked_shapes = [
        (D_MODEL, D_IN_A), (D_MODEL, D_IN_B), (1, LANES), (CONV_WIDTH, 2 * D_MLSTM),
        (1, D_MLSTM), (len(POOL_WINDOWS), POOL_GROUP_DIM, POOL_GROUP_DIM), (1, D_POOL),
        (1, D_MODEL), (1, D_MODEL), (1, D_MODEL), (1, D_MODEL),
    ]
    x_spec = pl.BlockSpec((ts, D_MODEL), lambda i, layer: (jnp.minimum(i, n_tiles - 1), 0))
    o_spec = pl.BlockSpec((ts, D_MODEL), lambda i, layer: (jnp.maximum(i - 1, 0), 0))

    def slab_in(rows, cols):
        return pl.BlockSpec(
            (None, rows // n_tiles, cols),
            lambda i, layer: (jnp.minimum(layer[0] + 1, DEPTH - 1), jnp.minimum(i, n_tiles - 1), 0))

    def slab_out(rows, cols):
        return pl.BlockSpec((rows // n_tiles, cols),
                            lambda i, layer: (jnp.minimum(i, n_tiles - 1), 0))

    big = [(D_MODEL, D_MODEL), (D_MODEL, D_FF), (D_FF, D_MODEL)]
    grid_spec = pltpu.PrefetchScalarGridSpec(
        num_scalar_prefetch=1,
        grid=(n_tiles + 1,),
        in_specs=([x_spec] + [_stacked_spec(s) for s in stacked_shapes]
                  + [_whole_spec(s) for s in big]
                  + [_whole_spec((chunk, chunk)), _whole_spec((chunk, chunk))]
                  + [slab_in(*s) for s in big]),
        out_specs=[o_spec] + [slab_out(*s) for s in big],
        scratch_shapes=[
            pltpu.VMEM((ts, D_MODEL), F32),
            pltpu.VMEM((ts + CONV_HALO, 2 * D_MLSTM), F32),
            pltpu.VMEM((ts + POOL_HALO, D_POOL), F32),
            pltpu.VMEM((ts, D_MLSTM), BF16),
            pltpu.VMEM((ts, D_MLSTM), F32),
            pltpu.VMEM((ts, LANES), F32),
            pltpu.VMEM((LANES, ts), F32),
            pltpu.VMEM((ts, D_MODEL), BF16),
            pltpu.VMEM((N_HEADS, HEAD_DIM, 2 * HEAD_DIM), F32),
            pltpu.VMEM((8, LANES), F32),
            pltpu.VMEM((ts, D_FF), BF16),
        ])
    return pl.pallas_call(
        functools.partial(_layer_kernel, ts=ts, chunk=chunk, nt=seq // ts, n_tiles=n_tiles),
        grid_spec=grid_spec,
        out_shape=[jax.ShapeDtypeStruct(x2d.shape, F32)]
        + [jax.ShapeDtypeStruct(s, BF16) for s in big],
        compiler_params=pltpu.CompilerParams(
            dimension_semantics=("arbitrary",),
            vmem_limit_bytes=VMEM_LIMIT_BYTES),
        name="layer",
    )(jnp.full((1,), layer, jnp.int32), x2d, *stacked, wout_b, w1_b, w2_b, *consts, *next_f32)


def kernel(x, w_in, b_gate, w_conv, hn_g, w_pool, pool_scale, w_out,
           ln1_g, ln1_b, w_ff1, w_ff2, ln2_g, ln2_b):
    batch, seq, d_model = x.shape
    assert d_model == D_MODEL and seq % SEQ_TILE == 0 and SEQ_TILE % MLSTM_CHUNK == 0
    assert D_MODEL % (16 * (batch * seq // SEQ_TILE)) == 0
    w_in_b = lax.optimization_barrier(w_in.astype(BF16))
    win_a = w_in_b[:, :, :D_IN_A]
    win_b = jnp.pad(w_in_b[:, :, D_IN_A:], ((0, 0), (0, 0), (GATE_LANE, 0)))
    stacked = (
        win_a, win_b,
        jnp.pad(b_gate, ((0, 0), (GATE_LANE, 0)))[:, None, :],
        w_conv, hn_g[:, None, :], w_pool.astype(BF16), pool_scale[:, None, :],
        ln1_g[:, None, :], ln1_b[:, None, :], ln2_g[:, None, :], ln2_b[:, None, :],
    )
    causal = jnp.tril(jnp.ones((MLSTM_CHUNK, MLSTM_CHUNK), dtype=bool))
    consts = (causal.astype(BF16), jnp.where(causal, 0.0, -jnp.inf).astype(F32))
    next_f32 = (w_out, w_ff1, w_ff2)
    big_b = (w_out[0].astype(BF16), w_ff1[0].astype(BF16), w_ff2[0].astype(BF16))
    h = x.reshape(batch * seq, d_model)
    for layer in range(DEPTH):
        h, *big_b = _layer_call(h, stacked, *big_b, consts, next_f32, layer, seq=seq)
    return h.reshape(batch, seq, d_model)
```

```python
import functools

import jax
import jax.numpy as jnp
from jax import lax
from jax.experimental import pallas as pl
from jax.experimental.pallas import tpu as pltpu

D_MODEL = 1024
DEPTH = 4
N_HEADS = 4
HEAD_DIM = 128
D_MLSTM = N_HEADS * HEAD_DIM
POOL_WINDOWS = (2, 4, 8, 16)
POOL_GROUP_DIM = 128
D_POOL = len(POOL_WINDOWS) * POOL_GROUP_DIM
CONV_WIDTH = 4
D_FF = 4 * D_MODEL
ALPHA = (2.0 * DEPTH) ** 0.25
LN_EPS = 1e-5

LANES = 128
N_GATES = 2 * N_HEADS
OFF_Q = 0
OFF_K = OFF_Q + D_MLSTM
OFF_V = OFF_K + D_MLSTM
D_IN_A = OFF_V + D_MLSTM
GATE_LANE = LANES - N_GATES
OFF_O = LANES
OFF_P = OFF_O + D_MLSTM
D_IN_B = OFF_P + D_POOL

CONV_HALO = 8
POOL_HALO = 16

SEQ_TILE = 512
MLSTM_CHUNK = 256
FF_CHUNK = 1024
DOWN_GROUP = 256
VMEM_LIMIT_BYTES = 60000 * 1024

F32 = jnp.float32
BF16 = jnp.bfloat16

PHASE_ORDER = "XMM" + "XXM" + "XM" * 2 + "XXM" * 4 + "XXXXX" + "M"


def _layer_norm_rows(z, g, b):
    mu = jnp.mean(z, axis=-1, keepdims=True)
    zc = z - mu
    var = jnp.mean(zc * zc, axis=-1, keepdims=True)
    return zc * lax.rsqrt(var + LN_EPS) * g + b


def _cumsum_rows(tri, x):
    p1 = x.astype(BF16)
    r1 = x - p1.astype(F32)
    p2 = r1.astype(BF16)
    p3 = (r1 - p2.astype(F32)).astype(BF16)
    out = jnp.dot(tri, p1, preferred_element_type=F32)
    out = out + jnp.dot(tri, p2, preferred_element_type=F32)
    return out + jnp.dot(tri, p3, preferred_element_type=F32)


def _mlp_phases(z1_buf, g1_ref, b1_ref, w1_ref, w2_ref, g2_ref, b2_ref, act_buf, o_ref):
    x1 = _layer_norm_rows(z1_buf[...], g1_ref[...], b1_ref[...])
    xb = x1.astype(BF16)
    yield

    def act_of(hid):
        return jnp.square(jnp.maximum(hid, 0.0)).astype(BF16)

    n_up = D_FF // FF_CHUNK
    hid_prev = None
    for c in range(n_up):
        hid = jnp.dot(xb, w1_ref[:, c * FF_CHUNK:(c + 1) * FF_CHUNK],
                      preferred_element_type=F32)
        if hid_prev is not None:
            act_buf[:, (c - 1) * FF_CHUNK:c * FF_CHUNK] = act_of(hid_prev)
        hid_prev = hid
        yield
    act_buf[:, (n_up - 1) * FF_CHUNK:] = act_of(hid_prev)

    outs = []
    for j in range(D_MODEL // DOWN_GROUP):
        outs.append(jnp.dot(act_buf[...], w2_ref[:, j * DOWN_GROUP:(j + 1) * DOWN_GROUP],
                            preferred_element_type=F32))
        yield
    y = jnp.concatenate(outs, axis=1)
    o_ref[...] = _layer_norm_rows(ALPHA * x1 + y, g2_ref[...], b2_ref[...])
    yield


def _mixer_phases(x_ref, t_glob, wina_ref, winb_ref, bg_ref, wconv_ref, hng_ref, wpool_ref,
                  pscale_ref, wout_ref, tri_ref, bias_ref, z1_buf,
                  qk_ext, p_ext, vb_ref, o_buf, z_ref, zt_ref, ycat, ct_ref, m_ref,
                  *, ts, chunk):
    n_chunks = ts // chunk
    xb = x_ref[...].astype(BF16)

    qk_ext[CONV_HALO:CONV_HALO + ts, :] = jnp.dot(
        xb, wina_ref[:, OFF_Q:OFF_V], preferred_element_type=F32)
    yield
    vb_ref[...] = jnp.dot(xb, wina_ref[:, OFF_V:D_IN_A], preferred_element_type=F32).astype(BF16)
    u_go = jnp.dot(xb, winb_ref[:, :OFF_P], preferred_element_type=F32)
    o_buf[...] = u_go[:, OFF_O:]
    gates = u_go[:, :LANES] + bg_ref[...]
    yield
    p_ext[POOL_HALO:POOL_HALO + ts, :] = jnp.dot(
        xb, winb_ref[:, OFF_P:D_IN_B], preferred_element_type=F32)
    logf = jax.nn.log_sigmoid(gates)
    lane = lax.broadcasted_iota(jnp.int32, (chunk, LANES), 1)
    for c in range(n_chunks):
        rs = slice(c * chunk, (c + 1) * chunk)
        a_c = _cumsum_rows(tri_ref[...], logf[rs, :])
        z_c = jnp.where(lane < GATE_LANE + N_HEADS, gates[rs, :], a_c)
        z_ref[rs, :] = z_c
        zt_ref[:, rs] = z_c.T
    yield

    wc = wconv_ref[...]
    conv_tail = qk_ext[ts:ts + CONV_HALO, :]
    for c in reversed(range(n_chunks)):
        blk = qk_ext[c * chunk:c * chunk + chunk + CONV_HALO, :]
        y = wc[0:1, :] * pltpu.roll(blk, CONV_WIDTH - 1, 0)[CONV_HALO:, :]
        for j in range(1, CONV_WIDTH - 1):
            y = y + wc[j:j + 1, :] * pltpu.roll(blk, CONV_WIDTH - 1 - j, 0)[CONV_HALO:, :]
        y = y + wc[CONV_WIDTH - 1:CONV_WIDTH, :] * blk[CONV_HALO:, :]
        act = jax.nn.silu(y)
        rows = slice(CONV_HALO + c * chunk, CONV_HALO + (c + 1) * chunk)
        qk_ext[rows, OFF_Q:OFF_K] = act[:, OFF_Q:OFF_K]
        qk_ext[rows, OFF_K:OFF_V] = act[:, OFF_K:OFF_V] * (HEAD_DIM ** -0.5)
        yield
    qk_ext[0:CONV_HALO, :] = conv_tail

    ones_col = jnp.where(lane == 0, 1.0, 0.0).astype(BF16)
    for c in range(n_chunks):
        rs = slice(c * chunk, (c + 1) * chunk)
        rows = slice(CONV_HALO + c * chunk, CONV_HALO + (c + 1) * chunk)
        for h in range(N_HEADS):
            hs = slice(h * HEAD_DIM, (h + 1) * HEAD_DIM)
            li, la = GATE_LANE + h, GATE_LANE + N_HEADS + h
            icol = z_ref[rs, li:li + 1]
            acol = z_ref[rs, la:la + 1]
            irow = zt_ref[li:li + 1, rs]
            arow = zt_ref[la:la + 1, rs]
            gtot = acol[chunk - 1:chunk, :]
            m_prev = m_ref[h:h + 1, 0:1]
            ct_prev = ct_ref[h]

            q_b = qk_ext[rows, OFF_Q + h * HEAD_DIM:OFF_Q + (h + 1) * HEAD_DIM].astype(BF16)
            k_h = qk_ext[rows, OFF_K + h * HEAD_DIM:OFF_K + (h + 1) * HEAD_DIM]
            k_b = k_h.astype(BF16)
            v_aug = jnp.concatenate([vb_ref[rs, hs], ones_col], axis=1)

            w_row = gtot - arow + irow
            m_loc = jnp.max(w_row, axis=1, keepdims=True)
            e_col = jnp.exp(gtot - acol + icol - m_loc)

            dlog = acol - arow + irow + bias_ref[...]
            m_inter = acol + m_prev
            m_t = jnp.maximum(m_inter, jnp.max(dlog, axis=1, keepdims=True))
            p = jnp.exp(dlog - m_t)
            s_inter = jnp.exp(m_inter - m_t)
            s = lax.dot_general(q_b, k_b, (((1,), (1,)), ((), ())), preferred_element_type=F32)
            sp_b = (s * p).astype(BF16)
            tot = (jnp.dot(sp_b, v_aug, preferred_element_type=F32)
                   + s_inter * jnp.dot(q_b, ct_prev.astype(BF16), preferred_element_type=F32))
            num = tot[:, :HEAD_DIM]
            den = tot[:, HEAD_DIM:HEAD_DIM + 1]
            hh = num / jnp.maximum(jnp.abs(den), jnp.exp(-m_t))

            mu = jnp.mean(hh, axis=1, keepdims=True)
            hc = hh - mu
            var = jnp.mean(hc * hc, axis=1, keepdims=True)
            hn = hc * lax.rsqrt(var + LN_EPS) * hng_ref[:, hs]
            ycat[rs, hs] = (jax.nn.sigmoid(o_buf[rs, hs]) * hn).astype(BF16)

            m_new = jnp.maximum(gtot + m_prev, m_loc)
            s_old = jnp.exp(gtot + m_prev - m_new)
            s_new = jnp.exp(m_loc - m_new)
            ke_t = (e_col * k_h).T.astype(BF16)
            ct_ref[h] = s_old * ct_prev + s_new * jnp.dot(ke_t, v_aug, preferred_element_type=F32)
            m_ref[h:h + 1, :] = jnp.broadcast_to(m_new, (1, LANES))
            yield

    ext = p_ext[...]
    p_ext[0:POOL_HALO, :] = ext[ts:ts + POOL_HALO, :]
    count = (t_glob * ts + 1 + lax.broadcasted_iota(jnp.int32, (ts, 1), 0)).astype(F32)
    wsum = ext
    shift = 1
    for gi, win in enumerate(POOL_WINDOWS):
        while shift < win:
            wsum = wsum + pltpu.roll(wsum, shift, 0)
            shift *= 2
        ls = slice(gi * POOL_GROUP_DIM, (gi + 1) * POOL_GROUP_DIM)
        cur = ext[POOL_HALO:, ls]
        d = wsum[POOL_HALO:, :POOL_GROUP_DIM] / jnp.minimum(count, float(win)) - cur
        yp = jnp.dot(d.astype(BF16), wpool_ref[gi], preferred_element_type=F32) * pscale_ref[:, ls]
        ycat[:, D_MLSTM + gi * POOL_GROUP_DIM:D_MLSTM + (gi + 1) * POOL_GROUP_DIM] = yp.astype(BF16)
        wsum = wsum[:, POOL_GROUP_DIM:]
        yield

    mix = jnp.dot(ycat[...], wout_ref[...], preferred_element_type=F32)
    z1_buf[...] = ALPHA * x_ref[...] + mix
    yield


def _layer_kernel(layer_ref, x_ref, wina_ref, winb_ref, bg_ref, wconv_ref, hng_ref, wpool_ref,
                  pscale_ref, g1_ref, b1_ref, g2_ref, b2_ref, wout_ref, w1_ref, w2_ref,
                  tri_ref, bias_ref,
                  nwout_ref, nw1_ref, nw2_ref,
                  o_ref, nwout_b_ref, nw1_b_ref, nw2_b_ref,
                  z1_buf, qk_ext, p_ext, vb_ref, o_buf, z_ref, zt_ref, ycat, ct_ref, m_ref,
                  act_buf, *, ts, chunk, nt, n_tiles):
    i = pl.program_id(0)
    t_glob = lax.rem(i, nt)

    @pl.when(t_glob == 0)
    def _():
        qk_ext[0:CONV_HALO, :] = jnp.zeros((CONV_HALO, 2 * D_MLSTM), F32)
        p_ext[0:POOL_HALO, :] = jnp.zeros((POOL_HALO, D_POOL), F32)
        ct_ref[...] = jnp.zeros_like(ct_ref)
        m_ref[...] = jnp.zeros_like(m_ref)

    def run(order):
        nwout_b_ref[...] = nwout_ref[...].astype(BF16)
        nw1_b_ref[...] = nw1_ref[...].astype(BF16)
        nw2_b_ref[...] = nw2_ref[...].astype(BF16)
        stages = {
            "X": _mixer_phases(
                x_ref, t_glob, wina_ref, winb_ref, bg_ref, wconv_ref, hng_ref, wpool_ref,
                pscale_ref, wout_ref, tri_ref, bias_ref, z1_buf, qk_ext, p_ext, vb_ref, o_buf,
                z_ref, zt_ref, ycat, ct_ref, m_ref, ts=ts, chunk=chunk),
            "M": _mlp_phases(z1_buf, g1_ref, b1_ref, w1_ref, w2_ref, g2_ref, b2_ref, act_buf,
                             o_ref),
        }
        for tag in order:
            next(stages[tag])
        done = object()
        for tag in set(order):
            assert next(stages[tag], done) is done

    pl.when(jnp.logical_and(i > 0, i < n_tiles))(lambda: run(PHASE_ORDER))
    pl.when(i == 0)(lambda: run(PHASE_ORDER.replace("M", "")))
    pl.when(i == n_tiles)(lambda: run(PHASE_ORDER.replace("X", "")))


def _stacked_spec(shape):
    return pl.BlockSpec((None,) + shape, lambda i, layer: (layer[0],) + (0,) * len(shape),
                        pipeline_mode=pl.Buffered(1))


def _whole_spec(shape):
    return pl.BlockSpec(shape, lambda i, layer: (0,) * len(shape),
                        pipeline_mode=pl.Buffered(1))


def _layer_call(x2d, stacked, wout_b, w1_b, w2_b, consts, next_f32, layer, *, seq):
    ts = SEQ_TILE
    chunk = MLSTM_CHUNK
    n_tiles = x2d.shape[0] // ts
    stacked_shapes = [
        (D_MODEL, D_IN_A), (D_MODEL, D_IN_B), (1, LANES), (CONV_WIDTH, 2 * D_MLSTM),
        (1, D_MLSTM), (len(POOL_WINDOWS), POOL_GROUP_DIM, POOL_GROUP_DIM), (1, D_POOL),
        (1, D_MODEL), (1, D_MODEL), (1, D_MODEL), (1, D_MODEL),
    ]
    x_spec = pl.BlockSpec((ts, D_MODEL), lambda i, layer: (jnp.minimum(i, n_tiles - 1), 0))
    o_spec = pl.BlockSpec((ts, D_MODEL), lambda i, layer: (jnp.maximum(i - 1, 0), 0))

    def slab_in(rows, cols):
        return pl.BlockSpec(
            (None, rows // n_tiles, cols),
            lambda i, layer: (jnp.minimum(layer[0] + 1, DEPTH - 1), jnp.minimum(i, n_tiles - 1), 0))

    def slab_out(rows, cols):
        return pl.BlockSpec((rows // n_tiles, cols),
                            lambda i, layer: (jnp.minimum(i, n_tiles - 1), 0))

    big = [(D_MODEL, D_MODEL), (D_MODEL, D_FF), (D_FF, D_MODEL)]
    grid_spec = pltpu.PrefetchScalarGridSpec(
        num_scalar_prefetch=1,
        grid=(n_tiles + 1,),
        in_specs=([x_spec] + [_stacked_spec(s) for s in stacked_shapes]
                  + [_whole_spec(s) for s in big]
                  + [_whole_spec((chunk, chunk)), _whole_spec((chunk, chunk))]
                  + [slab_in(*s) for s in big]),
        out_specs=[o_spec] + [slab_out(*s) for s in big],
        scratch_shapes=[
            pltpu.VMEM((ts, D_MODEL), F32),
            pltpu.VMEM((ts + CONV_HALO, 2 * D_MLSTM), F32),
            pltpu.VMEM((ts + POOL_HALO, D_POOL), F32),
            pltpu.VMEM((ts, D_MLSTM), BF16),
            pltpu.VMEM((ts, D_MLSTM), F32),
            pltpu.VMEM((ts, LANES), F32),
            pltpu.VMEM((LANES, ts), F32),
            pltpu.VMEM((ts, D_MODEL), BF16),
            pltpu.VMEM((N_HEADS, HEAD_DIM, 2 * HEAD_DIM), F32),
            pltpu.VMEM((8, LANES), F32),
            pltpu.VMEM((ts, D_FF), BF16),
        ])
    return pl.pallas_call(
        functools.partial(_layer_kernel, ts=ts, chunk=chunk, nt=seq // ts, n_tiles=n_tiles),
        grid_spec=grid_spec,
        out_shape=[jax.ShapeDtypeStruct(x2d.shape, F32)]
        + [jax.ShapeDtypeStruct(s, BF16) for s in big],
        compiler_params=pltpu.CompilerParams(
            dimension_semantics=("arbitrary",),
            vmem_limit_bytes=VMEM_LIMIT_BYTES),
        name="layer",
    )(jnp.full((1,), layer, jnp.int32), x2d, *stacked, wout_b, w1_b, w2_b, *consts, *next_f32)


def kernel(x, w_in, b_gate, w_conv, hn_g, w_pool, pool_scale, w_out,
           ln1_g, ln1_b, w_ff1, w_ff2, ln2_g, ln2_b):
    batch, seq, d_model = x.shape
    assert d_model == D_MODEL and seq % SEQ_TILE == 0 and SEQ_TILE % MLSTM_CHUNK == 0
    assert D_MODEL % (16 * (batch * seq // SEQ_TILE)) == 0
    win_a = w_in[:, :, :D_IN_A].astype(BF16)
    win_b = jnp.pad(w_in[:, :, D_IN_A:].astype(BF16), ((0, 0), (0, 0), (GATE_LANE, 0)))
    stacked = (
        win_a, win_b,
        jnp.pad(b_gate, ((0, 0), (GATE_LANE, 0)))[:, None, :],
        w_conv, hn_g[:, None, :], w_pool.astype(BF16), pool_scale[:, None, :],
        ln1_g[:, None, :], ln1_b[:, None, :], ln2_g[:, None, :], ln2_b[:, None, :],
    )
    causal = jnp.tril(jnp.ones((MLSTM_CHUNK, MLSTM_CHUNK), dtype=bool))
    consts = (causal.astype(BF16), jnp.where(causal, 0.0, -jnp.inf).astype(F32))
    next_f32 = (w_out, w_ff1, w_ff2)
    big_b = (w_out[0].astype(BF16), w_ff1[0].astype(BF16), w_ff2[0].astype(BF16))
    h = x.reshape(batch * seq, d_model)
    for layer in range(DEPTH):
        h, *big_b = _layer_call(h, stacked, *big_b, consts, next_f32, layer, seq=seq)
    return h.reshape(batch, seq, d_model)
```

```python
import functools

import jax
import jax.numpy as jnp
from jax import lax
from jax.experimental import pallas as pl
from jax.experimental.pallas import tpu as pltpu

D_MODEL = 1024
DEPTH = 4
N_HEADS = 4
HEAD_DIM = 128
D_MLSTM = N_HEADS * HEAD_DIM
POOL_WINDOWS = (2, 4, 8, 16)
POOL_GROUP_DIM = 128
D_POOL = len(POOL_WINDOWS) * POOL_GROUP_DIM
CONV_WIDTH = 4
D_FF = 4 * D_MODEL
ALPHA = (2.0 * DEPTH) ** 0.25
LN_EPS = 1e-5

LANES = 128
N_GATES = 2 * N_HEADS
OFF_Q = 0
OFF_K = OFF_Q + D_MLSTM
OFF_V = OFF_K + D_MLSTM
D_IN_A = OFF_V + D_MLSTM
GATE_LANE = LANES - N_GATES
OFF_O = LANES
OFF_P = OFF_O + D_MLSTM
D_IN_B = OFF_P + D_POOL

CONV_HALO = 8
POOL_HALO = 16

SEQ_TILE = 512
MLSTM_CHUNK = 256
FF_CHUNK = 1024
DOWN_GROUP = 256
VMEM_LIMIT_BYTES = 60000 * 1024

F32 = jnp.float32
BF16 = jnp.bfloat16

PHASE_ORDER = "XMM" + "XXM" + "XM" * 2 + "XXM" * 4 + "XXXXX" + "M"


def _layer_norm_rows(z, g, b):
    mu = jnp.mean(z, axis=-1, keepdims=True)
    zc = z - mu
    var = jnp.mean(zc * zc, axis=-1, keepdims=True)
    return zc * lax.rsqrt(var + LN_EPS) * g + b


def _cumsum_rows(tri, x):
    p1 = x.astype(BF16)
    r1 = x - p1.astype(F32)
    p2 = r1.astype(BF16)
    p3 = (r1 - p2.astype(F32)).astype(BF16)
    out = jnp.dot(tri, p1, preferred_element_type=F32)
    out = out + jnp.dot(tri, p2, preferred_element_type=F32)
    return out + jnp.dot(tri, p3, preferred_element_type=F32)


def _mlp_phases(z1_buf, g1_ref, b1_ref, w1_ref, w2_ref, g2_ref, b2_ref, act_buf, o_ref):
    x1 = _layer_norm_rows(z1_buf[...], g1_ref[...], b1_ref[...])
    xb = x1.astype(BF16)
    yield

    def act_of(hid):
        return jnp.square(jnp.maximum(hid, 0.0)).astype(BF16)

    n_up = D_FF // FF_CHUNK
    hid_prev = None
    for c in range(n_up):
        hid = jnp.dot(xb, w1_ref[:, c * FF_CHUNK:(c + 1) * FF_CHUNK],
                      preferred_element_type=F32)
        if hid_prev is not None:
            act_buf[:, (c - 1) * FF_CHUNK:c * FF_CHUNK] = act_of(hid_prev)
        hid_prev = hid
        yield
    act_buf[:, (n_up - 1) * FF_CHUNK:] = act_of(hid_prev)

    outs = []
    for j in range(D_MODEL // DOWN_GROUP):
        outs.append(jnp.dot(act_buf[...], w2_ref[:, j * DOWN_GROUP:(j + 1) * DOWN_GROUP],
                            preferred_element_type=F32))
        yield
    y = jnp.concatenate(outs, axis=1)
    o_ref[...] = _layer_norm_rows(ALPHA * x1 + y, g2_ref[...], b2_ref[...])
    yield


def _mixer_phases(x_ref, t_glob, wina_ref, winb_ref, bg_ref, wconv_ref, hng_ref, wpool_ref,
                  pscale_ref, wout_ref, tri_ref, bias_ref, z1_buf,
                  qk_ext, p_ext, vb_ref, o_buf, z_ref, zt_ref, ycat, ct_ref, m_ref,
                  *, ts, chunk):
    n_chunks = ts // chunk
    xb = x_ref[...].astype(BF16)

    qk_ext[CONV_HALO:CONV_HALO + ts, :] = jnp.dot(
        xb, wina_ref[:, OFF_Q:OFF_V], preferred_element_type=F32)
    yield
    vb_ref[...] = jnp.dot(xb, wina_ref[:, OFF_V:D_IN_A], preferred_element_type=F32).astype(BF16)
    u_go = jnp.dot(xb, winb_ref[:, :OFF_P], preferred_element_type=F32)
    o_buf[...] = u_go[:, OFF_O:]
    gates = u_go[:, :LANES] + bg_ref[...]
    yield
    p_ext[POOL_HALO:POOL_HALO + ts, :] = jnp.dot(
        xb, winb_ref[:, OFF_P:D_IN_B], preferred_element_type=F32)
    logf = jax.nn.log_sigmoid(gates)
    lane = lax.broadcasted_iota(jnp.int32, (chunk, LANES), 1)
    for c in range(n_chunks):
        rs = slice(c * chunk, (c + 1) * chunk)
        a_c = _cumsum_rows(tri_ref[...], logf[rs, :])
        z_c = jnp.where(lane < GATE_LANE + N_HEADS, gates[rs, :], a_c)
        z_ref[rs, :] = z_c
        zt_ref[:, rs] = z_c.T
    yield

    wc = wconv_ref[...]
    conv_tail = qk_ext[ts:ts + CONV_HALO, :]
    for c in reversed(range(n_chunks)):
        blk = qk_ext[c * chunk:c * chunk + chunk + CONV_HALO, :]
        y = wc[0:1, :] * pltpu.roll(blk, CONV_WIDTH - 1, 0)[CONV_HALO:, :]
        for j in range(1, CONV_WIDTH - 1):
            y = y + wc[j:j + 1, :] * pltpu.roll(blk, CONV_WIDTH - 1 - j, 0)[CONV_HALO:, :]
        y = y + wc[CONV_WIDTH - 1:CONV_WIDTH, :] * blk[CONV_HALO:, :]
        act = jax.nn.silu(y)
        rows = slice(CONV_HALO + c * chunk, CONV_HALO + (c + 1) * chunk)
        qk_ext[rows, OFF_Q:OFF_K] = act[:, OFF_Q:OFF_K]
        qk_ext[rows, OFF_K:OFF_V] = act[:, OFF_K:OFF_V] * (HEAD_DIM ** -0.5)
        yield
    qk_ext[0:CONV_HALO, :] = conv_tail

    ones_col = jnp.where(lane == 0, 1.0, 0.0).astype(BF16)
    for c in range(n_chunks):
        rs = slice(c * chunk, (c + 1) * chunk)
        rows = slice(CONV_HALO + c * chunk, CONV_HALO + (c + 1) * chunk)
        for h in range(N_HEADS):
            hs = slice(h * HEAD_DIM, (h + 1) * HEAD_DIM)
            li, la = GATE_LANE + h, GATE_LANE + N_HEADS + h
            icol = z_ref[rs, li:li + 1]
            acol = z_ref[rs, la:la + 1]
            irow = zt_ref[li:li + 1, rs]
            arow = zt_ref[la:la + 1, rs]
            gtot = acol[chunk - 1:chunk, :]
            m_prev = m_ref[h:h + 1, 0:1]
            ct_prev = ct_ref[h]

            q_h = qk_ext[rows, OFF_Q + h * HEAD_DIM:OFF_Q + (h + 1) * HEAD_DIM]
            q_b = q_h.astype(BF16)
            k_h = qk_ext[rows, OFF_K + h * HEAD_DIM:OFF_K + (h + 1) * HEAD_DIM]
            k_b = k_h.astype(BF16)
            v_aug = jnp.concatenate([vb_ref[rs, hs], ones_col], axis=1)

            w_row = gtot - arow + irow
            m_loc = jnp.max(w_row, axis=1, keepdims=True)
            e_col = jnp.exp(gtot - acol + icol - m_loc)

            dlog = acol - arow + irow + bias_ref[...]
            m_inter = acol + m_prev
            m_t = jnp.maximum(m_inter, jnp.max(dlog, axis=1, keepdims=True))
            p = jnp.exp(dlog - m_t)
            s_inter = jnp.exp(m_inter - m_t)
            s = lax.dot_general(q_b, k_b, (((1,), (1,)), ((), ())), preferred_element_type=F32)
            lhs = jnp.concatenate([(s * p).astype(BF16), (s_inter * q_h).astype(BF16)], axis=1)
            rhs = jnp.concatenate([v_aug, ct_prev.astype(BF16)], axis=0)
            tot = jnp.dot(lhs, rhs, preferred_element_type=F32)
            num = tot[:, :HEAD_DIM]
            den = tot[:, HEAD_DIM:HEAD_DIM + 1]
            hh = num / jnp.maximum(jnp.abs(den), jnp.exp(-m_t))

            mu = jnp.mean(hh, axis=1, keepdims=True)
            hc = hh - mu
            var = jnp.mean(hc * hc, axis=1, keepdims=True)
            hn = hc * lax.rsqrt(var + LN_EPS) * hng_ref[:, hs]
            ycat[rs, hs] = (jax.nn.sigmoid(o_buf[rs, hs]) * hn).astype(BF16)

            m_new = jnp.maximum(gtot + m_prev, m_loc)
            s_old = jnp.exp(gtot + m_prev - m_new)
            s_new = jnp.exp(m_loc - m_new)
            ke_t = (e_col * k_h).T.astype(BF16)
            ct_ref[h] = s_old * ct_prev + s_new * jnp.dot(ke_t, v_aug, preferred_element_type=F32)
            m_ref[h:h + 1, :] = jnp.broadcast_to(m_new, (1, LANES))
            yield

    ext = p_ext[...]
    p_ext[0:POOL_HALO, :] = ext[ts:ts + POOL_HALO, :]
    count = (t_glob * ts + 1 + lax.broadcasted_iota(jnp.int32, (ts, 1), 0)).astype(F32)
    wsum = ext
    shift = 1
    for gi, win in enumerate(POOL_WINDOWS):
        while shift < win:
            wsum = wsum + pltpu.roll(wsum, shift, 0)
            shift *= 2
        ls = slice(gi * POOL_GROUP_DIM, (gi + 1) * POOL_GROUP_DIM)
        cur = ext[POOL_HALO:, ls]
        d = wsum[POOL_HALO:, :POOL_GROUP_DIM] / jnp.minimum(count, float(win)) - cur
        yp = jnp.dot(d.astype(BF16), wpool_ref[gi], preferred_element_type=F32) * pscale_ref[:, ls]
        ycat[:, D_MLSTM + gi * POOL_GROUP_DIM:D_MLSTM + (gi + 1) * POOL_GROUP_DIM] = yp.astype(BF16)
        wsum = wsum[:, POOL_GROUP_DIM:]
        yield

    mix = jnp.dot(ycat[...], wout_ref[...], preferred_element_type=F32)
    z1_buf[...] = ALPHA * x_ref[...] + mix
    yield


def _layer_kernel(layer_ref, x_ref, wina_ref, winb_ref, bg_ref, wconv_ref, hng_ref, wpool_ref,
                  pscale_ref, g1_ref, b1_ref, g2_ref, b2_ref, wout_ref, w1_ref, w2_ref,
                  tri_ref, bias_ref,
                  nwout_ref, nw1_ref, nw2_ref,
                  o_ref, nwout_b_ref, nw1_b_ref, nw2_b_ref,
                  z1_buf, qk_ext, p_ext, vb_ref, o_buf, z_ref, zt_ref, ycat, ct_ref, m_ref,
                  act_buf, *, ts, chunk, nt, n_tiles):
    i = pl.program_id(0)
    t_glob = lax.rem(i, nt)

    @pl.when(t_glob == 0)
    def _():
        qk_ext[0:CONV_HALO, :] = jnp.zeros((CONV_HALO, 2 * D_MLSTM), F32)
        p_ext[0:POOL_HALO, :] = jnp.zeros((POOL_HALO, D_POOL), F32)
        ct_ref[...] = jnp.zeros_like(ct_ref)
        m_ref[...] = jnp.zeros_like(m_ref)

    nwout_b_ref[...] = nwout_ref[...].astype(BF16)
    nw1_b_ref[...] = nw1_ref[...].astype(BF16)
    nw2_b_ref[...] = nw2_ref[...].astype(BF16)

    def run(order):
        stages = {
            "X": _mixer_phases(
                x_ref, t_glob, wina_ref, winb_ref, bg_ref, wconv_ref, hng_ref, wpool_ref,
                pscale_ref, wout_ref, tri_ref, bias_ref, z1_buf, qk_ext, p_ext, vb_ref, o_buf,
                z_ref, zt_ref, ycat, ct_ref, m_ref, ts=ts, chunk=chunk),
            "M": _mlp_phases(z1_buf, g1_ref, b1_ref, w1_ref, w2_ref, g2_ref, b2_ref, act_buf,
                             o_ref),
        }
        for tag in order:
            next(stages[tag])
        done = object()
        for tag in set(order):
            assert next(stages[tag], done) is done

    pl.when(jnp.logical_and(i > 0, i < n_tiles))(lambda: run(PHASE_ORDER))
    pl.when(i == 0)(lambda: run(PHASE_ORDER.replace("M", "")))
    pl.when(i == n_tiles)(lambda: run(PHASE_ORDER.replace("X", "")))


def _stacked_spec(shape):
    return pl.BlockSpec((None,) + shape, lambda i, layer: (layer[0],) + (0,) * len(shape),
                        pipeline_mode=pl.Buffered(1))


def _whole_spec(shape):
    return pl.BlockSpec(shape, lambda i, layer: (0,) * len(shape),
                        pipeline_mode=pl.Buffered(1))


def _layer_call(x2d, stacked, wout_b, w1_b, w2_b, consts, next_f32, layer, *, seq):
    ts = SEQ_TILE
    chunk = MLSTM_CHUNK
    n_tiles = x2d.shape[0] // ts
    stacked_shapes = [
        (D_MODEL, D_IN_A), (D_MODEL, D_IN_B), (1, LANES), (CONV_WIDTH, 2 * D_MLSTM),
        (1, D_MLSTM), (len(POOL_WINDOWS), POOL_GROUP_DIM, POOL_GROUP_DIM), (1, D_POOL),
        (1, D_MODEL), (1, D_MODEL), (1, D_MODEL), (1, D_MODEL),
    ]
    x_spec = pl.BlockSpec((ts, D_MODEL), lambda i, layer: (jnp.minimum(i, n_tiles - 1), 0))
    o_spec = pl.BlockSpec((ts, D_MODEL), lambda i, layer: (jnp.maximum(i - 1, 0), 0))

    def slab_in(rows, cols):
        return pl.BlockSpec(
            (None, rows // n_tiles, cols),
            lambda i, layer: (jnp.minimum(layer[0] + 1, DEPTH - 1), jnp.minimum(i, n_tiles - 1), 0))

    def slab_out(rows, cols):
        return pl.BlockSpec((rows // n_tiles, cols),
                            lambda i, layer: (jnp.minimum(i, n_tiles - 1), 0))

    big = [(D_MODEL, D_MODEL), (D_MODEL, D_FF), (D_FF, D_MODEL)]
    grid_spec = pltpu.PrefetchScalarGridSpec(
        num_scalar_prefetch=1,
        grid=(n_tiles + 1,),
        in_specs=([x_spec] + [_stacked_spec(s) for s in stacked_shapes]
                  + [_whole_spec(s) for s in big]
                  + [_whole_spec((chunk, chunk)), _whole_spec((chunk, chunk))]
                  + [slab_in(*s) for s in big]),
        out_specs=[o_spec] + [slab_out(*s) for s in big],
        scratch_shapes=[
            pltpu.VMEM((ts, D_MODEL), F32),
            pltpu.VMEM((ts + CONV_HALO, 2 * D_MLSTM), F32),
            pltpu.VMEM((ts + POOL_HALO, D_POOL), F32),
            pltpu.VMEM((ts, D_MLSTM), BF16),
            pltpu.VMEM((ts, D_MLSTM), F32),
            pltpu.VMEM((ts, LANES), F32),
            pltpu.VMEM((LANES, ts), F32),
            pltpu.VMEM((ts, D_MODEL), BF16),
            pltpu.VMEM((N_HEADS, HEAD_DIM, 2 * HEAD_DIM), F32),
            pltpu.VMEM((8, LANES), F32),
            pltpu.VMEM((ts, D_FF), BF16),
        ])
    return pl.pallas_call(
        functools.partial(_layer_kernel, ts=ts, chunk=chunk, nt=seq // ts, n_tiles=n_tiles),
        grid_spec=grid_spec,
        out_shape=[jax.ShapeDtypeStruct(x2d.shape, F32)]
        + [jax.ShapeDtypeStruct(s, BF16) for s in big],
        compiler_params=pltpu.CompilerParams(
            dimension_semantics=("arbitrary",),
            vmem_limit_bytes=VMEM_LIMIT_BYTES),
        name="layer",
    )(jnp.full((1,), layer, jnp.int32), x2d, *stacked, wout_b, w1_b, w2_b, *consts, *next_f32)


def kernel(x, w_in, b_gate, w_conv, hn_g, w_pool, pool_scale, w_out,
           ln1_g, ln1_b, w_ff1, w_ff2, ln2_g, ln2_b):
    batch, seq, d_model = x.shape
    assert d_model == D_MODEL and seq % SEQ_TILE == 0 and SEQ_TILE % MLSTM_CHUNK == 0
    assert D_MODEL % (16 * (batch * seq // SEQ_TILE)) == 0
    win_a = w_in[:, :, :D_IN_A].astype(BF16)
    win_b = jnp.pad(w_in[:, :, D_IN_A:].astype(BF16), ((0, 0), (0, 0), (GATE_LANE, 0)))
    stacked = (
        win_a, win_b,
        jnp.pad(b_gate, ((0, 0), (GATE_LANE, 0)))[:, None, :],
        w_conv, hn_g[:, None, :], w_pool.astype(BF16), pool_scale[:, None, :],
        ln1_g[:, None, :], ln1_b[:, None, :], ln2_g[:, None, :], ln2_b[:, None, :],
    )
    causal = jnp.tril(jnp.ones((MLSTM_CHUNK, MLSTM_CHUNK), dtype=bool))
    consts = (causal.astype(BF16), jnp.where(causal, 0.0, -jnp.inf).astype(F32))
    next_f32 = (w_out, w_ff1, w_ff2)
    big_b = (w_out[0].astype(BF16), w_ff1[0].astype(BF16), w_ff2[0].astype(BF16))
    h = x.reshape(batch * seq, d_model)
    for layer in range(DEPTH):
        h, *big_b = _layer_call(h, stacked, *big_b, consts, next_f32, layer, seq=seq)
    return h.reshape(batch, seq, d_model)
```

```python
import functools

import jax
import jax.numpy as jnp
from jax import lax
from jax.experimental import pallas as pl
from jax.experimental.pallas import tpu as pltpu

D_MODEL = 1024
DEPTH = 4
N_HEADS = 4
HEAD_DIM = 128
D_MLSTM = N_HEADS * HEAD_DIM
POOL_WINDOWS = (2, 4, 8, 16)
POOL_GROUP_DIM = 128
D_POOL = len(POOL_WINDOWS) * POOL_GROUP_DIM
CONV_WIDTH = 4
D_FF = 4 * D_MODEL
ALPHA = (2.0 * DEPTH) ** 0.25
LN_EPS = 1e-5

LANES = 128
N_GATES = 2 * N_HEADS
OFF_Q = 0
OFF_K = OFF_Q + D_MLSTM
OFF_V = OFF_K + D_MLSTM
D_IN_A = OFF_V + D_MLSTM
GATE_LANE = LANES - N_GATES
OFF_O = LANES
OFF_P = OFF_O + D_MLSTM
D_IN_B = OFF_P + D_POOL

CONV_HALO = 8
POOL_HALO = 16

SEQ_TILE = 512
MLSTM_CHUNK = 256
FF_CHUNK = 1024
DOWN_GROUP = 256
VMEM_LIMIT_BYTES = 60000 * 1024

F32 = jnp.float32
BF16 = jnp.bfloat16

PHASE_ORDER = "XMM" + "XXM" + "XM" * 2 + "XXM" * 4 + "XXM" + "XXX"


def _layer_norm_rows(z, g, b):
    mu = jnp.mean(z, axis=-1, keepdims=True)
    zc = z - mu
    var = jnp.mean(zc * zc, axis=-1, keepdims=True)
    return zc * lax.rsqrt(var + LN_EPS) * g + b


def _cumsum_rows(tri, x):
    p1 = x.astype(BF16)
    r1 = x - p1.astype(F32)
    p2 = r1.astype(BF16)
    p3 = (r1 - p2.astype(F32)).astype(BF16)
    out = jnp.dot(tri, p1, preferred_element_type=F32)
    out = out + jnp.dot(tri, p2, preferred_element_type=F32)
    return out + jnp.dot(tri, p3, preferred_element_type=F32)


def _mlp_phases(z1_buf, g1_ref, b1_ref, w1_ref, w2_ref, g2_ref, b2_ref, act_buf, o_ref):
    x1 = _layer_norm_rows(z1_buf[...], g1_ref[...], b1_ref[...])
    xb = x1.astype(BF16)
    yield

    def act_of(hid):
        return jnp.square(jnp.maximum(hid, 0.0)).astype(BF16)

    n_up = D_FF // FF_CHUNK
    hid_prev = None
    for c in range(n_up):
        hid = jnp.dot(xb, w1_ref[:, c * FF_CHUNK:(c + 1) * FF_CHUNK],
                      preferred_element_type=F32)
        if hid_prev is not None:
            act_buf[:, (c - 1) * FF_CHUNK:c * FF_CHUNK] = act_of(hid_prev)
        hid_prev = hid
        yield
    act_buf[:, (n_up - 1) * FF_CHUNK:] = act_of(hid_prev)

    outs = []
    for j in range(D_MODEL // DOWN_GROUP):
        outs.append(jnp.dot(act_buf[...], w2_ref[:, j * DOWN_GROUP:(j + 1) * DOWN_GROUP],
                            preferred_element_type=F32))
        yield
    y = jnp.concatenate(outs, axis=1)
    o_ref[...] = _layer_norm_rows(ALPHA * x1 + y, g2_ref[...], b2_ref[...])
    yield


def _mixer_phases(x_ref, t_glob, wina_ref, winb_ref, bg_ref, wconv_ref, hng_ref, wpool_ref,
                  pscale_ref, wout_ref, tri_ref, bias_ref, z1_buf,
                  qk_ext, p_ext, vb_ref, o_buf, z_ref, zt_ref, ycat, ct_ref, m_ref,
                  *, ts, chunk):
    n_chunks = ts // chunk
    xb = x_ref[...].astype(BF16)

    qk_ext[CONV_HALO:CONV_HALO + ts, :] = jnp.dot(
        xb, wina_ref[:, OFF_Q:OFF_V], preferred_element_type=F32)
    yield
    vb_ref[...] = jnp.dot(xb, wina_ref[:, OFF_V:D_IN_A], preferred_element_type=F32).astype(BF16)
    u_go = jnp.dot(xb, winb_ref[:, :OFF_P], preferred_element_type=F32)
    o_buf[...] = u_go[:, OFF_O:]
    gates = u_go[:, :LANES] + bg_ref[...]
    yield
    p_ext[POOL_HALO:POOL_HALO + ts, :] = jnp.dot(
        xb, winb_ref[:, OFF_P:D_IN_B], preferred_element_type=F32)
    logf = jax.nn.log_sigmoid(gates)
    lane = lax.broadcasted_iota(jnp.int32, (chunk, LANES), 1)
    for c in range(n_chunks):
        rs = slice(c * chunk, (c + 1) * chunk)
        a_c = _cumsum_rows(tri_ref[...], logf[rs, :])
        z_c = jnp.where(lane < GATE_LANE + N_HEADS, gates[rs, :], a_c)
        z_ref[rs, :] = z_c
        zt_ref[:, rs] = z_c.T
    yield

    wc = wconv_ref[...]
    conv_tail = qk_ext[ts:ts + CONV_HALO, :]
    for c in reversed(range(n_chunks)):
        blk = qk_ext[c * chunk:c * chunk + chunk + CONV_HALO, :]
        y = wc[0:1, :] * pltpu.roll(blk, CONV_WIDTH - 1, 0)[CONV_HALO:, :]
        for j in range(1, CONV_WIDTH - 1):
            y = y + wc[j:j + 1, :] * pltpu.roll(blk, CONV_WIDTH - 1 - j, 0)[CONV_HALO:, :]
        y = y + wc[CONV_WIDTH - 1:CONV_WIDTH, :] * blk[CONV_HALO:, :]
        act = jax.nn.silu(y)
        rows = slice(CONV_HALO + c * chunk, CONV_HALO + (c + 1) * chunk)
        qk_ext[rows, OFF_Q:OFF_K] = act[:, OFF_Q:OFF_K]
        qk_ext[rows, OFF_K:OFF_V] = act[:, OFF_K:OFF_V] * (HEAD_DIM ** -0.5)
        yield
    qk_ext[0:CONV_HALO, :] = conv_tail

    ones_col = jnp.where(lane == 0, 1.0, 0.0).astype(BF16)
    for c in range(n_chunks):
        rs = slice(c * chunk, (c + 1) * chunk)
        rows = slice(CONV_HALO + c * chunk, CONV_HALO + (c + 1) * chunk)
        for h in range(N_HEADS):
            hs = slice(h * HEAD_DIM, (h + 1) * HEAD_DIM)
            li, la = GATE_LANE + h, GATE_LANE + N_HEADS + h
            icol = z_ref[rs, li:li + 1]
            acol = z_ref[rs, la:la + 1]
            irow = zt_ref[li:li + 1, rs]
            arow = zt_ref[la:la + 1, rs]
            gtot = acol[chunk - 1:chunk, :]
            m_prev = m_ref[h:h + 1, 0:1]
            ct_prev = ct_ref[h]

            q_b = qk_ext[rows, OFF_Q + h * HEAD_DIM:OFF_Q + (h + 1) * HEAD_DIM].astype(BF16)
            k_h = qk_ext[rows, OFF_K + h * HEAD_DIM:OFF_K + (h + 1) * HEAD_DIM]
            k_b = k_h.astype(BF16)
            v_aug = jnp.concatenate([vb_ref[rs, hs], ones_col], axis=1)

            w_row = gtot - arow + irow
            m_loc = jnp.max(w_row, axis=1, keepdims=True)
            e_col = jnp.exp(gtot - acol + icol - m_loc)

            dlog = acol - arow + irow + bias_ref[...]
            m_inter = acol + m_prev
            m_t = jnp.maximum(m_inter, jnp.max(dlog, axis=1, keepdims=True))
            p = jnp.exp(dlog - m_t)
            s_inter = jnp.exp(m_inter - m_t)
            s = lax.dot_general(q_b, k_b, (((1,), (1,)), ((), ())), preferred_element_type=F32)
            sp_b = (s * p).astype(BF16)
            tot = (jnp.dot(sp_b, v_aug, preferred_element_type=F32)
                   + s_inter * jnp.dot(q_b, ct_prev.astype(BF16), preferred_element_type=F32))
            num = tot[:, :HEAD_DIM]
            den = tot[:, HEAD_DIM:HEAD_DIM + 1]
            hh = num / jnp.maximum(jnp.abs(den), jnp.exp(-m_t))

            mu = jnp.mean(hh, axis=1, keepdims=True)
            hc = hh - mu
            var = jnp.mean(hc * hc, axis=1, keepdims=True)
            hn = hc * lax.rsqrt(var + LN_EPS) * hng_ref[:, hs]
            ycat[rs, hs] = (jax.nn.sigmoid(o_buf[rs, hs]) * hn).astype(BF16)

            m_new = jnp.maximum(gtot + m_prev, m_loc)
            s_old = jnp.exp(gtot + m_prev - m_new)
            s_new = jnp.exp(m_loc - m_new)
            ke_t = (e_col * k_h).T.astype(BF16)
            ct_ref[h] = s_old * ct_prev + s_new * jnp.dot(ke_t, v_aug, preferred_element_type=F32)
            m_ref[h:h + 1, :] = jnp.broadcast_to(m_new, (1, LANES))
            yield

    ext = p_ext[...]
    p_ext[0:POOL_HALO, :] = ext[ts:ts + POOL_HALO, :]
    count = (t_glob * ts + 1 + lax.broadcasted_iota(jnp.int32, (ts, 1), 0)).astype(F32)
    wsum = ext
    shift = 1
    for gi, win in enumerate(POOL_WINDOWS):
        while shift < win:
            wsum = wsum + pltpu.roll(wsum, shift, 0)
            shift *= 2
        ls = slice(gi * POOL_GROUP_DIM, (gi + 1) * POOL_GROUP_DIM)
        cur = ext[POOL_HALO:, ls]
        d = wsum[POOL_HALO:, :POOL_GROUP_DIM] / jnp.minimum(count, float(win)) - cur
        yp = jnp.dot(d.astype(BF16), wpool_ref[gi], preferred_element_type=F32) * pscale_ref[:, ls]
        ycat[:, D_MLSTM + gi * POOL_GROUP_DIM:D_MLSTM + (gi + 1) * POOL_GROUP_DIM] = yp.astype(BF16)
        wsum = wsum[:, POOL_GROUP_DIM:]
        yield

    mix = jnp.dot(ycat[...], wout_ref[...], preferred_element_type=F32)
    z1_buf[...] = ALPHA * x_ref[...] + mix
    yield


def _layer_kernel(layer_ref, x_ref, wina_ref, winb_ref, bg_ref, wconv_ref, hng_ref, wpool_ref,
                  pscale_ref, g1_ref, b1_ref, g2_ref, b2_ref, wout_ref, w1_ref, w2_ref,
                  tri_ref, bias_ref,
                  nwout_ref, nw1_ref, nw2_ref,
                  o_ref, nwout_b_ref, nw1_b_ref, nw2_b_ref,
                  z1_buf, qk_ext, p_ext, vb_ref, o_buf, z_ref, zt_ref, ycat, ct_ref, m_ref,
                  act_buf, *, ts, chunk, nt, n_tiles):
    i = pl.program_id(0)
    t_glob = lax.rem(i, nt)

    @pl.when(t_glob == 0)
    def _():
        qk_ext[0:CONV_HALO, :] = jnp.zeros((CONV_HALO, 2 * D_MLSTM), F32)
        p_ext[0:POOL_HALO, :] = jnp.zeros((POOL_HALO, D_POOL), F32)
        ct_ref[...] = jnp.zeros_like(ct_ref)
        m_ref[...] = jnp.zeros_like(m_ref)

    nwout_b_ref[...] = nwout_ref[...].astype(BF16)
    nw1_b_ref[...] = nw1_ref[...].astype(BF16)
    nw2_b_ref[...] = nw2_ref[...].astype(BF16)

    def run(order):
        stages = {
            "X": _mixer_phases(
                x_ref, t_glob, wina_ref, winb_ref, bg_ref, wconv_ref, hng_ref, wpool_ref,
                pscale_ref, wout_ref, tri_ref, bias_ref, z1_buf, qk_ext, p_ext, vb_ref, o_buf,
                z_ref, zt_ref, ycat, ct_ref, m_ref, ts=ts, chunk=chunk),
            "M": _mlp_phases(z1_buf, g1_ref, b1_ref, w1_ref, w2_ref, g2_ref, b2_ref, act_buf,
                             o_ref),
        }
        for tag in order:
            next(stages[tag])
        done = object()
        for tag in set(order):
            assert next(stages[tag], done) is done

    pl.when(jnp.logical_and(i > 0, i < n_tiles))(lambda: run(PHASE_ORDER))
    pl.when(i == 0)(lambda: run(PHASE_ORDER.replace("M", "")))
    pl.when(i == n_tiles)(lambda: run(PHASE_ORDER.replace("X", "")))


def _stacked_spec(shape):
    return pl.BlockSpec((None,) + shape, lambda i, layer: (layer[0],) + (0,) * len(shape),
                        pipeline_mode=pl.Buffered(1))


def _whole_spec(shape):
    return pl.BlockSpec(shape, lambda i, layer: (0,) * len(shape),
                        pipeline_mode=pl.Buffered(1))


def _layer_call(x2d, stacked, wout_b, w1_b, w2_b, consts, next_f32, layer, *, seq):
    ts = SEQ_TILE
    chunk = MLSTM_CHUNK
    n_tiles = x2d.shape[0] // ts
    stacked_shapes = [
        (D_MODEL, D_IN_A), (D_MODEL, D_IN_B), (1, LANES), (CONV_WIDTH, 2 * D_MLSTM),
        (1, D_MLSTM), (len(POOL_WINDOWS), POOL_GROUP_DIM, POOL_GROUP_DIM), (1, D_POOL),
        (1, D_MODEL), (1, D_MODEL), (1, D_MODEL), (1, D_MODEL),
    ]
    x_spec = pl.BlockSpec((ts, D_MODEL), lambda i, layer: (jnp.minimum(i, n_tiles - 1), 0))
    o_spec = pl.BlockSpec((ts, D_MODEL), lambda i, layer: (jnp.maximum(i - 1, 0), 0))

    def slab_in(rows, cols):
        return pl.BlockSpec(
            (None, rows // n_tiles, cols),
            lambda i, layer: (jnp.minimum(layer[0] + 1, DEPTH - 1), jnp.minimum(i, n_tiles - 1), 0))

    def slab_out(rows, cols):
        return pl.BlockSpec((rows // n_tiles, cols),
                            lambda i, layer: (jnp.minimum(i, n_tiles - 1), 0))

    big = [(D_MODEL, D_MODEL), (D_MODEL, D_FF), (D_FF, D_MODEL)]
    grid_spec = pltpu.PrefetchScalarGridSpec(
        num_scalar_prefetch=1,
        grid=(n_tiles + 1,),
        in_specs=([x_spec] + [_stacked_spec(s) for s in stacked_shapes]
                  + [_whole_spec(s) for s in big]
                  + [_whole_spec((chunk, chunk)), _whole_spec((chunk, chunk))]
                  + [slab_in(*s) for s in big]),
        out_specs=[o_spec] + [slab_out(*s) for s in big],
        scratch_shapes=[
            pltpu.VMEM((ts, D_MODEL), F32),
            pltpu.VMEM((ts + CONV_HALO, 2 * D_MLSTM), F32),
            pltpu.VMEM((ts + POOL_HALO, D_POOL), F32),
            pltpu.VMEM((ts, D_MLSTM), BF16),
            pltpu.VMEM((ts, D_MLSTM), F32),
            pltpu.VMEM((ts, LANES), F32),
            pltpu.VMEM((LANES, ts), F32),
            pltpu.VMEM((ts, D_MODEL), BF16),
            pltpu.VMEM((N_HEADS, HEAD_DIM, 2 * HEAD_DIM), F32),
            pltpu.VMEM((8, LANES), F32),
            pltpu.VMEM((ts, D_FF), BF16),
        ])
    return pl.pallas_call(
        functools.partial(_layer_kernel, ts=ts, chunk=chunk, nt=seq // ts, n_tiles=n_tiles),
        grid_spec=grid_spec,
        out_shape=[jax.ShapeDtypeStruct(x2d.shape, F32)]
        + [jax.ShapeDtypeStruct(s, BF16) for s in big],
        compiler_params=pltpu.CompilerParams(
            dimension_semantics=("arbitrary",),
            vmem_limit_bytes=VMEM_LIMIT_BYTES),
        name="layer",
    )(jnp.full((1,), layer, jnp.int32), x2d, *stacked, wout_b, w1_b, w2_b, *consts, *next_f32)


def kernel(x, w_in, b_gate, w_conv, hn_g, w_pool, pool_scale, w_out,
           ln1_g, ln1_b, w_ff1, w_ff2, ln2_g, ln2_b):
    batch, seq, d_model = x.shape
    assert d_model == D_MODEL and seq % SEQ_TILE == 0 and SEQ_TILE % MLSTM_CHUNK == 0
    assert D_MODEL % (16 * (batch * seq // SEQ_TILE)) == 0
    win_a = w_in[:, :, :D_IN_A].astype(BF16)
    win_b = jnp.pad(w_in[:, :, D_IN_A:].astype(BF16), ((0, 0), (0, 0), (GATE_LANE, 0)))
    stacked = (
        win_a, win_b,
        jnp.pad(b_gate, ((0, 0), (GATE_LANE, 0)))[:, None, :],
        w_conv, hn_g[:, None, :], w_pool.astype(BF16), pool_scale[:, None, :],
        ln1_g[:, None, :], ln1_b[:, None, :], ln2_g[:, None, :], ln2_b[:, None, :],
    )
    causal = jnp.tril(jnp.ones((MLSTM_CHUNK, MLSTM_CHUNK), dtype=bool))
    consts = (causal.astype(BF16), jnp.where(causal, 0.0, -jnp.inf).astype(F32))
    next_f32 = (w_out, w_ff1, w_ff2)
    big_b = (w_out[0].astype(BF16), w_ff1[0].astype(BF16), w_ff2[0].astype(BF16))
    h = x.reshape(batch * seq, d_model)
    for layer in range(DEPTH):
        h, *big_b = _layer_call(h, stacked, *big_b, consts, next_f32, layer, seq=seq)
    return h.reshape(batch, seq, d_model)
```

```python
import functools

import jax
import jax.numpy as jnp
from jax import lax
from jax.experimental import pallas as pl
from jax.experimental.pallas import tpu as pltpu

D_MODEL = 1024
DEPTH = 4
N_HEADS = 4
HEAD_DIM = 128
D_MLSTM = N_HEADS * HEAD_DIM
POOL_WINDOWS = (2, 4, 8, 16)
POOL_GROUP_DIM = 128
D_POOL = len(POOL_WINDOWS) * POOL_GROUP_DIM
CONV_WIDTH = 4
D_FF = 4 * D_MODEL
ALPHA = (2.0 * DEPTH) ** 0.25
LN_EPS = 1e-5

LANES = 128
N_GATES = 2 * N_HEADS
OFF_Q = 0
OFF_K = OFF_Q + D_MLSTM
OFF_V = OFF_K + D_MLSTM
D_IN_A = OFF_V + D_MLSTM
GATE_LANE = LANES - N_GATES
OFF_O = LANES
OFF_P = OFF_O + D_MLSTM
D_IN_B = OFF_P + D_POOL

CONV_HALO = 8
POOL_HALO = 16

SEQ_TILE = 512
MLSTM_CHUNK = 256
FF_CHUNK = 1024
DOWN_GROUP = 256
VMEM_LIMIT_BYTES = 60000 * 1024

F32 = jnp.float32
BF16 = jnp.bfloat16

PHASE_ORDER = "XMM" + "XXM" + "XXM" * 2 + "XXM" * 4 + "XXXXX" + "M"


def _layer_norm_rows(z, g, b):
    mu = jnp.mean(z, axis=-1, keepdims=True)
    zc = z - mu
    var = jnp.mean(zc * zc, axis=-1, keepdims=True)
    return zc * lax.rsqrt(var + LN_EPS) * g + b


def _cumsum_rows(tri, x):
    p1 = x.astype(BF16)
    r1 = x - p1.astype(F32)
    p2 = r1.astype(BF16)
    p3 = (r1 - p2.astype(F32)).astype(BF16)
    out = jnp.dot(tri, p1, preferred_element_type=F32)
    out = out + jnp.dot(tri, p2, preferred_element_type=F32)
    return out + jnp.dot(tri, p3, preferred_element_type=F32)


def _mlp_phases(z1_buf, g1_ref, b1_ref, w1_ref, w2_ref, g2_ref, b2_ref, act_buf, o_ref):
    x1 = _layer_norm_rows(z1_buf[...], g1_ref[...], b1_ref[...])
    xb = x1.astype(BF16)
    yield

    def act_of(hid):
        return jnp.square(jnp.maximum(hid, 0.0)).astype(BF16)

    n_up = D_FF // FF_CHUNK
    hid_prev = None
    for c in range(n_up):
        hid = jnp.dot(xb, w1_ref[:, c * FF_CHUNK:(c + 1) * FF_CHUNK],
                      preferred_element_type=F32)
        if hid_prev is not None:
            act_buf[:, (c - 1) * FF_CHUNK:c * FF_CHUNK] = act_of(hid_prev)
        hid_prev = hid
        yield
    act_buf[:, (n_up - 1) * FF_CHUNK:] = act_of(hid_prev)

    outs = []
    for j in range(D_MODEL // DOWN_GROUP):
        outs.append(jnp.dot(act_buf[...], w2_ref[:, j * DOWN_GROUP:(j + 1) * DOWN_GROUP],
                            preferred_element_type=F32))
        yield
    y = jnp.concatenate(outs, axis=1)
    o_ref[...] = _layer_norm_rows(ALPHA * x1 + y, g2_ref[...], b2_ref[...])
    yield


def _mixer_phases(x_ref, t_glob, wina_ref, winb_ref, bg_ref, wconv_ref, hng_ref, wpool_ref,
                  pscale_ref, wout_ref, tri_ref, bias_ref, z1_buf,
                  qk_ext, p_ext, vb_ref, o_buf, z_ref, zt_ref, ycat, ct_ref, m_ref,
                  *, ts, chunk):
    n_chunks = ts // chunk
    xb = x_ref[...].astype(BF16)

    qk_ext[CONV_HALO:CONV_HALO + ts, :] = jnp.dot(
        xb, wina_ref[:, OFF_Q:OFF_V], preferred_element_type=F32)
    yield
    vb_ref[...] = jnp.dot(xb, wina_ref[:, OFF_V:D_IN_A], preferred_element_type=F32).astype(BF16)
    u_go = jnp.dot(xb, winb_ref[:, :OFF_P], preferred_element_type=F32)
    o_buf[...] = u_go[:, OFF_O:]
    gates = u_go[:, :LANES] + bg_ref[...]
    yield
    p_ext[POOL_HALO:POOL_HALO + ts, :] = jnp.dot(
        xb, winb_ref[:, OFF_P:D_IN_B], preferred_element_type=F32)
    logf = jax.nn.log_sigmoid(gates)
    lane = lax.broadcasted_iota(jnp.int32, (chunk, LANES), 1)
    for c in range(n_chunks):
        rs = slice(c * chunk, (c + 1) * chunk)
        a_c = _cumsum_rows(tri_ref[...], logf[rs, :])
        z_c = jnp.where(lane < GATE_LANE + N_HEADS, gates[rs, :], a_c)
        z_ref[rs, :] = z_c
        zt_ref[:, rs] = z_c.T
    yield

    wc = wconv_ref[...]
    conv_tail = qk_ext[ts:ts + CONV_HALO, :]
    for c in reversed(range(n_chunks)):
        rows = slice(CONV_HALO + c * chunk, CONV_HALO + (c + 1) * chunk)
        for cols, scale in ((slice(OFF_Q, OFF_K), 1.0), (slice(OFF_K, OFF_V), HEAD_DIM ** -0.5)):
            blk = qk_ext[c * chunk:c * chunk + chunk + CONV_HALO, cols]
            y = wc[0:1, cols] * pltpu.roll(blk, CONV_WIDTH - 1, 0)[CONV_HALO:, :]
            for j in range(1, CONV_WIDTH - 1):
                y = y + wc[j:j + 1, cols] * pltpu.roll(blk, CONV_WIDTH - 1 - j, 0)[CONV_HALO:, :]
            y = y + wc[CONV_WIDTH - 1:CONV_WIDTH, cols] * blk[CONV_HALO:, :]
            act = jax.nn.silu(y)
            qk_ext[rows, cols] = act if scale == 1.0 else act * scale
            yield
    qk_ext[0:CONV_HALO, :] = conv_tail

    ones_col = jnp.where(lane == 0, 1.0, 0.0).astype(BF16)
    for c in range(n_chunks):
        rs = slice(c * chunk, (c + 1) * chunk)
        rows = slice(CONV_HALO + c * chunk, CONV_HALO + (c + 1) * chunk)
        for h in range(N_HEADS):
            hs = slice(h * HEAD_DIM, (h + 1) * HEAD_DIM)
            li, la = GATE_LANE + h, GATE_LANE + N_HEADS + h
            icol = z_ref[rs, li:li + 1]
            acol = z_ref[rs, la:la + 1]
            irow = zt_ref[li:li + 1, rs]
            arow = zt_ref[la:la + 1, rs]
            gtot = acol[chunk - 1:chunk, :]
            m_prev = m_ref[h:h + 1, 0:1]
            ct_prev = ct_ref[h]

            q_b = qk_ext[rows, OFF_Q + h * HEAD_DIM:OFF_Q + (h + 1) * HEAD_DIM].astype(BF16)
            k_h = qk_ext[rows, OFF_K + h * HEAD_DIM:OFF_K + (h + 1) * HEAD_DIM]
            k_b = k_h.astype(BF16)
            v_aug = jnp.concatenate([vb_ref[rs, hs], ones_col], axis=1)

            w_row = gtot - arow + irow
            m_loc = jnp.max(w_row, axis=1, keepdims=True)
            e_col = jnp.exp(gtot - acol + icol - m_loc)

            dlog = acol - arow + irow + bias_ref[...]
            m_inter = acol + m_prev
            m_t = jnp.maximum(m_inter, jnp.max(dlog, axis=1, keepdims=True))
            p = jnp.exp(dlog - m_t)
            s_inter = jnp.exp(m_inter - m_t)
            s = lax.dot_general(q_b, k_b, (((1,), (1,)), ((), ())), preferred_element_type=F32)
            sp_b = (s * p).astype(BF16)
            tot = (jnp.dot(sp_b, v_aug, preferred_element_type=F32)
                   + s_inter * jnp.dot(q_b, ct_prev.astype(BF16), preferred_element_type=F32))
            num = tot[:, :HEAD_DIM]
            den = tot[:, HEAD_DIM:HEAD_DIM + 1]
            hh = num / jnp.maximum(jnp.abs(den), jnp.exp(-m_t))

            mu = jnp.mean(hh, axis=1, keepdims=True)
            hc = hh - mu
            var = jnp.mean(hc * hc, axis=1, keepdims=True)
            hn = hc * lax.rsqrt(var + LN_EPS) * hng_ref[:, hs]
            ycat[rs, hs] = (jax.nn.sigmoid(o_buf[rs, hs]) * hn).astype(BF16)

            m_new = jnp.maximum(gtot + m_prev, m_loc)
            s_old = jnp.exp(gtot + m_prev - m_new)
            s_new = jnp.exp(m_loc - m_new)
            ke_t = (e_col * k_h).T.astype(BF16)
            ct_ref[h] = s_old * ct_prev + s_new * jnp.dot(ke_t, v_aug, preferred_element_type=F32)
            m_ref[h:h + 1, :] = jnp.broadcast_to(m_new, (1, LANES))
            yield

    ext = p_ext[...]
    p_ext[0:POOL_HALO, :] = ext[ts:ts + POOL_HALO, :]
    count = (t_glob * ts + 1 + lax.broadcasted_iota(jnp.int32, (ts, 1), 0)).astype(F32)
    wsum = ext
    shift = 1
    for gi, win in enumerate(POOL_WINDOWS):
        while shift < win:
            wsum = wsum + pltpu.roll(wsum, shift, 0)
            shift *= 2
        ls = slice(gi * POOL_GROUP_DIM, (gi + 1) * POOL_GROUP_DIM)
        cur = ext[POOL_HALO:, ls]
        d = wsum[POOL_HALO:, :POOL_GROUP_DIM] / jnp.minimum(count, float(win)) - cur
        yp = jnp.dot(d.astype(BF16), wpool_ref[gi], preferred_element_type=F32) * pscale_ref[:, ls]
        ycat[:, D_MLSTM + gi * POOL_GROUP_DIM:D_MLSTM + (gi + 1) * POOL_GROUP_DIM] = yp.astype(BF16)
        wsum = wsum[:, POOL_GROUP_DIM:]
        yield

    mix = jnp.dot(ycat[...], wout_ref[...], preferred_element_type=F32)
    z1_buf[...] = ALPHA * x_ref[...] + mix
    yield


def _layer_kernel(layer_ref, x_ref, wina_ref, winb_ref, bg_ref, wconv_ref, hng_ref, wpool_ref,
                  pscale_ref, g1_ref, b1_ref, g2_ref, b2_ref, wout_ref, w1_ref, w2_ref,
                  tri_ref, bias_ref,
                  nwout_ref, nw1_ref, nw2_ref,
                  o_ref, nwout_b_ref, nw1_b_ref, nw2_b_ref,
                  z1_buf, qk_ext, p_ext, vb_ref, o_buf, z_ref, zt_ref, ycat, ct_ref, m_ref,
                  act_buf, *, ts, chunk, nt, n_tiles):
    i = pl.program_id(0)
    t_glob = lax.rem(i, nt)

    @pl.when(t_glob == 0)
    def _():
        qk_ext[0:CONV_HALO, :] = jnp.zeros((CONV_HALO, 2 * D_MLSTM), F32)
        p_ext[0:POOL_HALO, :] = jnp.zeros((POOL_HALO, D_POOL), F32)
        ct_ref[...] = jnp.zeros_like(ct_ref)
        m_ref[...] = jnp.zeros_like(m_ref)

    nwout_b_ref[...] = nwout_ref[...].astype(BF16)
    nw1_b_ref[...] = nw1_ref[...].astype(BF16)
    nw2_b_ref[...] = nw2_ref[...].astype(BF16)

    def run(order):
        stages = {
            "X": _mixer_phases(
                x_ref, t_glob, wina_ref, winb_ref, bg_ref, wconv_ref, hng_ref, wpool_ref,
                pscale_ref, wout_ref, tri_ref, bias_ref, z1_buf, qk_ext, p_ext, vb_ref, o_buf,
                z_ref, zt_ref, ycat, ct_ref, m_ref, ts=ts, chunk=chunk),
            "M": _mlp_phases(z1_buf, g1_ref, b1_ref, w1_ref, w2_ref, g2_ref, b2_ref, act_buf,
                             o_ref),
        }
        for tag in order:
            next(stages[tag])
        done = object()
        for tag in set(order):
            assert next(stages[tag], done) is done

    pl.when(jnp.logical_and(i > 0, i < n_tiles))(lambda: run(PHASE_ORDER))
    pl.when(i == 0)(lambda: run(PHASE_ORDER.replace("M", "")))
    pl.when(i == n_tiles)(lambda: run(PHASE_ORDER.replace("X", "")))


def _stacked_spec(shape):
    return pl.BlockSpec((None,) + shape, lambda i, layer: (layer[0],) + (0,) * len(shape),
                        pipeline_mode=pl.Buffered(1))


def _whole_spec(shape):
    return pl.BlockSpec(shape, lambda i, layer: (0,) * len(shape),
                        pipeline_mode=pl.Buffered(1))


def _layer_call(x2d, stacked, wout_b, w1_b, w2_b, consts, next_f32, layer, *, seq):
    ts = SEQ_TILE
    chunk = MLSTM_CHUNK
    n_tiles = x2d.shape[0] // ts
    stacked_shapes = [
        (D_MODEL, D_IN_A), (D_MODEL, D_IN_B), (1, LANES), (CONV_WIDTH, 2 * D_MLSTM),
        (1, D_MLSTM), (len(POOL_WINDOWS), POOL_GROUP_DIM, POOL_GROUP_DIM), (1, D_POOL),
        (1, D_MODEL), (1, D_MODEL), (1, D_MODEL), (1, D_MODEL),
    ]
    x_spec = pl.BlockSpec((ts, D_MODEL), lambda i, layer: (jnp.minimum(i, n_tiles - 1), 0))
    o_spec = pl.BlockSpec((ts, D_MODEL), lambda i, layer: (jnp.maximum(i - 1, 0), 0))

    def slab_in(rows, cols):
        return pl.BlockSpec(
            (None, rows // n_tiles, cols),
            lambda i, layer: (jnp.minimum(layer[0] + 1, DEPTH - 1), jnp.minimum(i, n_tiles - 1), 0))

    def slab_out(rows, cols):
        return pl.BlockSpec((rows // n_tiles, cols),
                            lambda i, layer: (jnp.minimum(i, n_tiles - 1), 0))

    big = [(D_MODEL, D_MODEL), (D_MODEL, D_FF), (D_FF, D_MODEL)]
    grid_spec = pltpu.PrefetchScalarGridSpec(
        num_scalar_prefetch=1,
        grid=(n_tiles + 1,),
        in_specs=([x_spec] + [_stacked_spec(s) for s in stacked_shapes]
                  + [_whole_spec(s) for s in big]
                  + [_whole_spec((chunk, chunk)), _whole_spec((chunk, chunk))]
                  + [slab_in(*s) for s in big]),
        out_specs=[o_spec] + [slab_out(*s) for s in big],
        scratch_shapes=[
            pltpu.VMEM((ts, D_MODEL), F32),
            pltpu.VMEM((ts + CONV_HALO, 2 * D_MLSTM), F32),
            pltpu.VMEM((ts + POOL_HALO, D_POOL), F32),
            pltpu.VMEM((ts, D_MLSTM), BF16),
            pltpu.VMEM((ts, D_MLSTM), F32),
            pltpu.VMEM((ts, LANES), F32),
            pltpu.VMEM((LANES, ts), F32),
            pltpu.VMEM((ts, D_MODEL), BF16),
            pltpu.VMEM((N_HEADS, HEAD_DIM, 2 * HEAD_DIM), F32),
            pltpu.VMEM((8, LANES), F32),
            pltpu.VMEM((ts, D_FF), BF16),
        ])
    return pl.pallas_call(
        functools.partial(_layer_kernel, ts=ts, chunk=chunk, nt=seq // ts, n_tiles=n_tiles),
        grid_spec=grid_spec,
        out_shape=[jax.ShapeDtypeStruct(x2d.shape, F32)]
        + [jax.ShapeDtypeStruct(s, BF16) for s in big],
        compiler_params=pltpu.CompilerParams(
            dimension_semantics=("arbitrary",),
            vmem_limit_bytes=VMEM_LIMIT_BYTES),
        name="layer",
    )(jnp.full((1,), layer, jnp.int32), x2d, *stacked, wout_b, w1_b, w2_b, *consts, *next_f32)


def kernel(x, w_in, b_gate, w_conv, hn_g, w_pool, pool_scale, w_out,
           ln1_g, ln1_b, w_ff1, w_ff2, ln2_g, ln2_b):
    batch, seq, d_model = x.shape
    assert d_model == D_MODEL and seq % SEQ_TILE == 0 and SEQ_TILE % MLSTM_CHUNK == 0
    assert D_MODEL % (16 * (batch * seq // SEQ_TILE)) == 0
    win_a = w_in[:, :, :D_IN_A].astype(BF16)
    win_b = jnp.pad(w_in[:, :, D_IN_A:].astype(BF16), ((0, 0), (0, 0), (GATE_LANE, 0)))
    stacked = (
        win_a, win_b,
        jnp.pad(b_gate, ((0, 0), (GATE_LANE, 0)))[:, None, :],
        w_conv, hn_g[:, None, :], w_pool.astype(BF16), pool_scale[:, None, :],
        ln1_g[:, None, :], ln1_b[:, None, :], ln2_g[:, None, :], ln2_b[:, None, :],
    )
    causal = jnp.tril(jnp.ones((MLSTM_CHUNK, MLSTM_CHUNK), dtype=bool))
    consts = (causal.astype(BF16), jnp.where(causal, 0.0, -jnp.inf).astype(F32))
    next_f32 = (w_out, w_ff1, w_ff2)
    big_b = (w_out[0].astype(BF16), w_ff1[0].astype(BF16), w_ff2[0].astype(BF16))
    h = x.reshape(batch * seq, d_model)
    for layer in range(DEPTH):
        h, *big_b = _layer_call(h, stacked, *big_b, consts, next_f32, layer, seq=seq)
    return h.reshape(batch, seq, d_model)
```
